```python
import math
import jax
import jax.numpy as jnp
from jax import lax
import numpy as np

D_MODEL = 2048
BATCH = 4
SEQ = 2048
DEPTH = 4

GRID_W = 64
CTX_LEN = 256
N_EVEN = (DEPTH + 1) // 2
N_ODD = DEPTH // 2
NORM_EPS = 1e-6
N_MOD = 6

HEAD_DIM = 128
A_Q_HEADS = (D_MODEL // 2) // HEAD_DIM
A_KV_HEADS = max(1, A_Q_HEADS // 4)
A_GROUP = A_Q_HEADS // A_KV_HEADS
A_Q_W = A_Q_HEADS * HEAD_DIM
A_KV_W = A_KV_HEADS * HEAD_DIM
WINDOW = 128
ATTN_BLOCK = 128
ROPE_THETA = 10000.0

CONV_CH = D_MODEL // 2
CONV_GROUPS = CONV_CH // 128
CONV_K = 31

EVEN_SPLITS = (A_Q_W, A_Q_W + A_KV_W, A_Q_W + 2 * A_KV_W)
EVEN_IN_W = A_Q_W + 2 * A_KV_W + 2 * CONV_CH
EVEN_CAT_W = A_Q_W + CONV_CH

DN_DK = 128
DN_DV = 128
DN_K_HEADS = D_MODEL // DN_DK
DN_V_HEADS = 2 * DN_K_HEADS
DN_QK_W = DN_K_HEADS * DN_DK
DN_V_W = DN_V_HEADS * DN_DV
DN_CONV_CH = 2 * DN_QK_W + DN_V_W
DN_BA_W = 4 * DN_V_HEADS
DN_SHORT_K = 5
DN_CHUNK = 64
DT_MIN = 0.001
DT_MAX = 0.1

D_FF = 256 * ((8 * D_MODEL // 3 + 255) // 256)
FFN_CONV_K = 3

kernel_name = 'hybrid_swa_conformer_gdn_convffn_diffusion'


def rms_norm(x, g):
    xf = x.astype(jnp.float32)
    y = xf * lax.rsqrt(jnp.mean(xf * xf, axis=-1, keepdims=True) + NORM_EPS)
    return (y * g.astype(jnp.float32)).astype(x.dtype)


def l2_norm(x):
    xf = x.astype(jnp.float32)
    return xf * lax.rsqrt(jnp.sum(xf * xf, axis=-1, keepdims=True) + NORM_EPS)


def dwconv(x, w, b=None):
    k, ch = w.shape
    left = (k - 1) // 2
    y = lax.conv_general_dilated(x, w[:, None, :].astype(x.dtype), window_strides=(1,),
                                 padding=[(left, k - 1 - left)],
                                 dimension_numbers=('NWC', 'WIO', 'NWC'),
                                 feature_group_count=ch)
    return y if b is None else y + b.astype(x.dtype)


def adaln(cond, w_mod, b_mod):
    return jnp.split(jax.nn.silu(cond) @ w_mod + b_mod, N_MOD, axis=-1)


def axial_rope(n_tokens):
    rows = n_tokens // GRID_W
    row = jnp.repeat(jnp.arange(rows, dtype=jnp.float32), GRID_W)
    col = jnp.tile(jnp.arange(GRID_W, dtype=jnp.float32), rows)
    half = HEAD_DIM // 2
    inv_freq = ROPE_THETA ** (-jnp.arange(0, half, 2, dtype=jnp.float32) / half)
    ang_r = row[:, None] * inv_freq
    ang_c = col[:, None] * inv_freq
    ang = jnp.concatenate([ang_r, ang_r, ang_c, ang_c], axis=-1)
    return jnp.cos(ang), jnp.sin(ang)


def apply_rope(x, cos, sin):
    xf = x.astype(jnp.float32)
    x1, x2, x3, x4 = jnp.split(xf, 4, axis=-1)
    rot = jnp.concatenate([-x2, x1, -x4, x3], axis=-1)
    return (xf * cos[:, None, :] + rot * sin[:, None, :]).astype(x.dtype)


def banded_latent_attention(q, k, v, k_ctx, v_ctx, sink):
    B, T, _, dh = q.shape
    L = k_ctx.shape[1]
    nb = T // ATTN_BLOCK
    scale = dh ** -0.5
    qb = q.reshape(B, nb, ATTN_BLOCK, A_KV_HEADS, A_GROUP, dh)
    pad = ((0, 0), (ATTN_BLOCK, ATTN_BLOCK), (0, 0), (0, 0))

    def band(t):
        tb = jnp.pad(t, pad).reshape(B, nb + 2, ATTN_BLOCK, A_KV_HEADS, dh)
        return jnp.concatenate([tb[:, :-2], tb[:, 1:-1], tb[:, 2:]], axis=2)

    kb, vb = band(k), band(v)
    s_band = jnp.einsum('bnqhgd,bnkhd->bnhgqk', qb, kb).astype(jnp.float32) * scale
    s_ctx = jnp.einsum('bnqhgd,bkhd->bnhgqk', qb, k_ctx).astype(jnp.float32) * scale
    qpos = jnp.arange(nb)[:, None] * ATTN_BLOCK + jnp.arange(ATTN_BLOCK)[None, :]
    kpos = jnp.arange(nb)[:, None] * ATTN_BLOCK - ATTN_BLOCK + jnp.arange(3 * ATTN_BLOCK)[None, :]
    rel = kpos[:, None, :] - qpos[:, :, None]
    valid = (jnp.abs(rel) <= WINDOW) & (kpos[:, None, :] >= 0) & (kpos[:, None, :] < T)
    s_band = jnp.where(valid[None, :, None, None], s_band, -jnp.inf)
    sink_l = jnp.broadcast_to(sink.astype(jnp.float32).reshape(A_KV_HEADS, A_GROUP)[:, :, None, None],
                              (B, nb, A_KV_HEADS, A_GROUP, ATTN_BLOCK, 1))
    p = jax.nn.softmax(jnp.concatenate([sink_l, s_ctx, s_band], axis=-1), axis=-1).astype(v.dtype)
    o = (jnp.einsum('bnhgqk,bkhd->bnqhgd', p[..., 1:1 + L], v_ctx)
         + jnp.einsum('bnhgqk,bnkhd->bnqhgd', p[..., 1 + L:], vb))
    return o.reshape(B, T, A_Q_HEADS * dh)


def context_attention(q, k, v, sink):
    B, L, _, dh = q.shape
    qg = q.reshape(B, L, A_KV_HEADS, A_GROUP, dh)
    s = jnp.einsum('bqhgd,bkhd->bhgqk', qg, k).astype(jnp.float32) * (dh ** -0.5)
    sink_l = jnp.broadcast_to(sink.astype(jnp.float32).reshape(A_KV_HEADS, A_GROUP)[:, :, None, None],
                              (B, A_KV_HEADS, A_GROUP, L, 1))
    p = jax.nn.softmax(jnp.concatenate([sink_l, s], axis=-1), axis=-1).astype(v.dtype)
    o = jnp.einsum('bhgqk,bkhd->bqhgd', p[..., 1:], v)
    return o.reshape(B, L, A_Q_HEADS * dh)


def conformer_conv(u, dw_w, dw_b, ln_g, ln_b):
    a, b = jnp.split(u, 2, axis=-1)
    h = dwconv(a * jax.nn.sigmoid(b), dw_w, dw_b)
    B, T, C = h.shape
    hf = h.astype(jnp.float32).reshape(B, T, CONV_GROUPS, C // CONV_GROUPS)
    mu = jnp.mean(hf, axis=-1, keepdims=True)
    var = jnp.mean(jnp.square(hf - mu), axis=-1, keepdims=True)
    hn = ((hf - mu) * lax.rsqrt(var + NORM_EPS)).reshape(B, T, C) * ln_g + ln_b
    return jax.nn.silu(hn).astype(u.dtype)


def even_mixer(nx, nc, cos, sin, w_in, w_out, q_g, k_g, sink, dw_w, dw_b, ln_g, ln_b, need_ctx):
    B, T, _ = nx.shape
    L = nc.shape[1]
    q, k, v, glu = jnp.split(nx @ w_in, EVEN_SPLITS, axis=-1)
    q = apply_rope(rms_norm(q.reshape(B, T, A_Q_HEADS, HEAD_DIM), q_g), cos, sin)
    k = apply_rope(rms_norm(k.reshape(B, T, A_KV_HEADS, HEAD_DIM), k_g), cos, sin)
    v = v.reshape(B, T, A_KV_HEADS, HEAD_DIM)
    if need_ctx:
        qc, kc, vc, gluc = jnp.split(nc @ w_in, EVEN_SPLITS, axis=-1)
    else:
        kc, vc = jnp.split(nc @ w_in[:, A_Q_W:A_Q_W + 2 * A_KV_W], 2, axis=-1)
    kc = rms_norm(kc.reshape(B, L, A_KV_HEADS, HEAD_DIM), k_g)
    vc = vc.reshape(B, L, A_KV_HEADS, HEAD_DIM)
    ox = jnp.concatenate([banded_latent_attention(q, k, v, kc, vc, sink),
                          conformer_conv(glu, dw_w, dw_b, ln_g, ln_b)], axis=-1) @ w_out
    if not need_ctx:
        return ox, None
    qc = rms_norm(qc.reshape(B, L, A_Q_HEADS, HEAD_DIM), q_g)
    oc = jnp.concatenate([context_attention(qc, kc, vc, sink),
                          conformer_conv(gluc, dw_w, dw_b, ln_g, ln_b)], axis=-1) @ w_out
    return ox, oc


def dn_gates(ba, a_log, dt_bias):
    b_f, b_b, a_f, a_b = jnp.split(ba.astype(jnp.float32), 4, axis=-1)
    beta = jax.nn.sigmoid(jnp.stack([b_f, b_b]))
    a = jnp.stack([a_f, a_b])
    g = -jnp.exp(a_log.astype(jnp.float32))[:, None, None, :] * jax.nn.softplus(
        a + dt_bias.astype(jnp.float32)[:, None, None, :])
    return beta, g


def dn_project(n, w_in, conv_w, a_log, dt_bias, with_qz):
    B, T, _ = n.shape
    rep = DN_V_HEADS // DN_K_HEADS
    if with_qz:
        u = n @ w_in[:, :DN_CONV_CH + DN_V_W]
        qkv = jax.nn.silu(dwconv(u[..., :DN_CONV_CH], conv_w))
        q = jnp.repeat(l2_norm(qkv[..., :DN_QK_W].reshape(B, T, DN_K_HEADS, DN_DK)), rep, axis=2) * (DN_DK ** -0.5)
        kv = qkv[..., DN_QK_W:]
        z = u[..., DN_CONV_CH:]
    else:
        kv = jax.nn.silu(dwconv(n @ w_in[:, DN_QK_W:DN_CONV_CH], conv_w[:, DN_QK_W:]))
        q = None
        z = None
    k = jnp.repeat(l2_norm(kv[..., :DN_QK_W].reshape(B, T, DN_K_HEADS, DN_DK)), rep, axis=2)
    v = kv[..., DN_QK_W:].reshape(B, T, DN_V_HEADS, DN_DV).astype(jnp.float32)
    beta, g = dn_gates(n @ w_in[:, DN_CONV_CH + DN_V_W:], a_log, dt_bias)
    return q, k, v, z, beta, g


def gated_delta_chunked(q, k, v, g, beta, s0):
    B, T, H, _ = k.shape
    n = T // DN_CHUNK

    def to_chunks(t):
        t = t.reshape((B, n, DN_CHUNK, H) + t.shape[3:])
        return jnp.moveaxis(t, (1, 3), (0, 2))

    idx = jnp.arange(DN_CHUNK)
    incl = idx[:, None] >= idx[None, :]
    strict = idx[:, None] > idx[None, :]
    gcum = jnp.cumsum(to_chunks(g), axis=-1)
    with_out = q is not None
    xs = (to_chunks(k), to_chunks(v), gcum, to_chunks(beta))
    if with_out:
        xs = xs + (to_chunks(q),)

    def step(S, xs_i):
        k_i, v_i, g_i, b_i = xs_i[:4]
        decay = jnp.exp(jnp.where(incl, g_i[..., :, None] - g_i[..., None, :], -jnp.inf))
        kb = k_i * b_i[..., None]
        lower = jnp.where(strict, jnp.einsum('bhid,bhjd->bhij', kb, k_i) * decay, 0.0)
        rhs = jnp.concatenate([v_i * b_i[..., None], kb * jnp.exp(g_i)[..., None]], axis=-1)
        sol = lax.linalg.triangular_solve(lower, rhs, left_side=True, lower=True, unit_diagonal=True)
        u_i, w_i = sol[..., :DN_DV], sol[..., DN_DV:]
        v_new = u_i - w_i @ S
        g_last = g_i[..., -1:]
        S_next = S * jnp.exp(g_last)[..., None] + jnp.einsum(
            'bhck,bhcv->bhkv', k_i * jnp.exp(g_last - g_i)[..., None], v_new)
        if not with_out:
            return S_next, None
        q_i = xs_i[4]
        o_i = (q_i * jnp.exp(g_i)[..., None]) @ S + (jnp.einsum('bhid,bhjd->bhij', q_i, k_i) * decay) @ v_new
        return S_next, o_i

    s_final, o = lax.scan(step, s0, xs)
    if not with_out:
        return None, s_final
    o = jnp.moveaxis(o, (0, 2), (1, 3)).reshape(B, T, H, DN_DV)
    return o, s_final


def rev(t):
    return None if t is None else jnp.flip(t, axis=1)


def dn_output(o, z, norm_g, w_out):
    B, T = o.shape[:2]
    o = o * lax.rsqrt(jnp.mean(o * o, axis=-1, keepdims=True) + NORM_EPS) * norm_g.astype(jnp.float32)
    o = o * jax.nn.silu(z.astype(jnp.float32).reshape(B, T, DN_V_HEADS, DN_DV))
    return o.reshape(B, T, DN_V_W).astype(z.dtype) @ w_out


def odd_mixer(nx, nc, w_in, conv_w, a_log, dt_bias, norm_g, w_out, need_ctx):
    B = nx.shape[0]
    s0 = jnp.zeros((B, DN_V_HEADS, DN_DK, DN_DV), jnp.float32)
    qx, kx, vx, zx, bx, gx = dn_project(nx, w_in, conv_w, a_log, dt_bias, True)
    qc, kc, vc, zc, bc, gc = dn_project(nc, w_in, conv_w, a_log, dt_bias, need_ctx)
    oc_f, sc_f = gated_delta_chunked(qc, kc, vc, gc[0], bc[0], s0)
    oc_b, sc_b = gated_delta_chunked(rev(qc), rev(kc), rev(vc), rev(gc[1]), rev(bc[1]), s0)
    ox_f, _ = gated_delta_chunked(qx, kx, vx, gx[0], bx[0], sc_f)
    ox_b, _ = gated_delta_chunked(rev(qx), rev(kx), rev(vx), rev(gx[1]), rev(bx[1]), sc_b)
    out_x = dn_output(ox_f + rev(ox_b), zx, norm_g, w_out)
    if not need_ctx:
        return out_x, None
    return out_x, dn_output(oc_f + rev(oc_b), zc, norm_g, w_out)


def conv_ffn(n, w_up, conv_w, conv_b, w_down):
    gate, val = jnp.split(n @ w_up, 2, axis=-1)
    return (jax.nn.silu(dwconv(gate, conv_w, conv_b)) * val) @ w_down


def setup_inputs(seed: int = 0) -> dict:
    key = jax.random.key(seed)
    ks = iter(jax.random.split(key, 40))
    f32 = jnp.float32
    D = D_MODEL

    def nrm(shape, std):
        return jax.random.normal(next(ks), shape, f32) * std

    def gain(shape):
        return 1.0 + nrm(shape, 0.01)

    dt = jnp.exp(jax.random.uniform(next(ks), (N_ODD, 2, DN_V_HEADS), f32)
                 * (math.log(DT_MAX) - math.log(DT_MIN)) + math.log(DT_MIN))
    return {
        'x': nrm((BATCH, SEQ, D), 1.0),
        'c': nrm((BATCH, D), 1.0),
        'ctx': nrm((BATCH, CTX_LEN, D), 1.0),
        'c_ctx': nrm((D,), 1.0),
        'w_mod': nrm((DEPTH, D, N_MOD * D), 0.5 * D ** -0.5),
        'b_mod': nrm((DEPTH, N_MOD * D), 0.01),
        'norm1_g': gain((DEPTH, D)),
        'norm2_g': gain((DEPTH, D)),
        'ffn_w_up': nrm((DEPTH, D, 2 * D_FF), D ** -0.5),
        'ffn_conv_w': nrm((DEPTH, FFN_CONV_K, D_FF), FFN_CONV_K ** -0.5),
        'ffn_conv_b': nrm((DEPTH, D_FF), 0.01),
        'ffn_w_down': nrm((DEPTH, D_FF, D), D_FF ** -0.5),
        'even_w_in': nrm((N_EVEN, D, EVEN_IN_W), D ** -0.5),
        'even_w_out': nrm((N_EVEN, EVEN_CAT_W, D), EVEN_CAT_W ** -0.5),
        'attn_q_norm_g': gain((N_EVEN, HEAD_DIM)),
        'attn_k_norm_g': gain((N_EVEN, HEAD_DIM)),
        'attn_sink': nrm((N_EVEN, A_Q_HEADS), 1.0),
        'conv_dw_w': nrm((N_EVEN, CONV_K, CONV_CH), CONV_K ** -0.5),
        'conv_dw_b': nrm((N_EVEN, CONV_CH), 0.01),
        'conv_ln_g': gain((N_EVEN, CONV_CH)),
        'conv_ln_b': nrm((N_EVEN, CONV_CH), 0.01),
        'dn_w_in': nrm((N_ODD, D, DN_CONV_CH + DN_V_W + DN_BA_W), D ** -0.5),
        'dn_conv_w': nrm((N_ODD, DN_SHORT_K, DN_CONV_CH), DN_SHORT_K ** -0.5),
        'dn_a_log': jnp.log(jax.random.uniform(next(ks), (N_ODD, 2, DN_V_HEADS), f32, 1.0, 16.0)),
        'dn_dt_bias': dt + jnp.log(-jnp.expm1(-dt)),
        'dn_norm_g': gain((N_ODD, DN_DV)),
        'dn_w_out': nrm((N_ODD, DN_V_W, D), DN_V_W ** -0.5),
    }


def reference(x, c, ctx, c_ctx, w_mod, b_mod, norm1_g, norm2_g, ffn_w_up, ffn_conv_w, ffn_conv_b,
              ffn_w_down, even_w_in, even_w_out, attn_q_norm_g, attn_k_norm_g, attn_sink, conv_dw_w,
              conv_dw_b, conv_ln_g, conv_ln_b, dn_w_in, dn_conv_w, dn_a_log, dn_dt_bias, dn_norm_g,
              dn_w_out):
    cos, sin = axial_rope(x.shape[1])
    hx, hc = x, ctx
    for layer in range(DEPTH):
        last = layer == DEPTH - 1
        i = layer // 2
        sh1, sc1, g1, sh2, sc2, g2 = [m[:, None, :] for m in adaln(c, w_mod[layer], b_mod[layer])]
        csh1, csc1, cg1, csh2, csc2, cg2 = adaln(c_ctx, w_mod[layer], b_mod[layer])
        nx = rms_norm(hx, norm1_g[layer]) * (1 + sc1) + sh1
        nc = rms_norm(hc, norm1_g[layer]) * (1 + csc1) + csh1
        if layer % 2 == 0:
            ox, oc = even_mixer(nx, nc, cos, sin, even_w_in[i], even_w_out[i], attn_q_norm_g[i],
                                attn_k_norm_g[i], attn_sink[i], conv_dw_w[i], conv_dw_b[i],
                                conv_ln_g[i], conv_ln_b[i], not last)
        else:
            ox, oc = odd_mixer(nx, nc, dn_w_in[i], dn_conv_w[i], dn_a_log[i], dn_dt_bias[i],
                               dn_norm_g[i], dn_w_out[i], not last)
        hx = hx + g1 * ox
        hx = hx + g2 * conv_ffn(rms_norm(hx, norm2_g[layer]) * (1 + sc2) + sh2, ffn_w_up[layer],
                                ffn_conv_w[layer], ffn_conv_b[layer], ffn_w_down[layer])
        if not last:
            hc = hc + cg1 * oc
            hc = hc + cg2 * conv_ffn(rms_norm(hc, norm2_g[layer]) * (1 + csc2) + csh2, ffn_w_up[layer],
                                     ffn_conv_w[layer], ffn_conv_b[layer], ffn_w_down[layer])
    return hx
```

```python
import functools
import math

import jax
import jax.numpy as jnp
from jax import lax
from jax.experimental import pallas as pl
from jax.experimental.pallas import tpu as pltpu

F32 = jnp.float32
BF16 = jnp.bfloat16

D = 2048
NB = 4
SEQ = 2048
CTX = 256
S = CTX + SEQ
DEPTH = 4
GRID_W = 64
EPS = 1e-6
N_MOD = 6

HEAD_DIM = 128
A_Q_HEADS = 8
A_KV_HEADS = 2
A_GROUP = 4
A_Q_W = 1024
A_KV_W = 256
WINDOW = 128
ABLK = 128
ROPE_THETA = 10000.0
CONV_CH = 1024
CONV_K = 31
EVEN_IN_W = 3584

DN_DK = 128
DN_DV = 128
DN_K_HEADS = 16
DN_V_HEADS = 32
DN_QK_W = 2048
DN_V_W = 4096
DN_CONV_CH = 8192
DN_BA_W = 128
DN_SHORT_K = 5
CHUNK = 64
NCH = S // CHUNK
CTX_CH = CTX // CHUNK

D_FF = 5632
FFN_CONV_K = 3

VMEM_LIMIT = 58 * 1024 * 1024


def _params(sem, vmem=VMEM_LIMIT):
    return pltpu.CompilerParams(dimension_semantics=sem, vmem_limit_bytes=vmem)


def _silu(x):
    return x * jax.nn.sigmoid(x)


def _dot(a, b):
    return jnp.dot(a, b, preferred_element_type=F32)


def _dot_nt(a, b):
    return lax.dot_general(a, b, (((1,), (1,)), ((), ())), preferred_element_type=F32)


def _dot_tn(a, b):
    return lax.dot_general(a, b, (((0,), (0,)), ((), ())), preferred_element_type=F32)


def _seg_shift(x, shift):
    row = lax.broadcasted_iota(jnp.int32, (S, 1), 0)
    y = pltpu.roll(x, (-shift) % S, 0)
    src = row + shift
    lo = jnp.where(row < CTX, 0, CTX)
    hi = jnp.where(row < CTX, CTX, S)
    return jnp.where((src >= lo) & (src < hi), y, 0.0)


def _mod_kernel(c_ref, w_ref, b_ref, o_ref):
    a = _silu(c_ref[...]).astype(BF16)
    o_ref[0] = _dot(a, w_ref[0].astype(BF16)) + b_ref[0]


def _modulation(cvec, w_mod, b_mod):
    tn = 1024
    n = N_MOD * D
    return pl.pallas_call(
        _mod_kernel,
        grid=(DEPTH, n // tn),
        in_specs=[
            pl.BlockSpec((8, D), lambda l, j: (0, 0)),
            pl.BlockSpec((1, D, tn), lambda l, j: (l, 0, j)),
            pl.BlockSpec((1, 1, tn), lambda l, j: (l, 0, j)),
        ],
        out_specs=pl.BlockSpec((1, 8, tn), lambda l, j: (l, 0, j)),
        out_shape=jax.ShapeDtypeStruct((DEPTH, 8, n), F32),
        compiler_params=_params(("parallel", "parallel")),
        name="adaln_mod",
    )(cvec, w_mod, b_mod.reshape(DEPTH, 1, n))


def _norm_mod_kernel(h_ref, g_ref, sc_ref, sh_ref, o_ref):
    x = h_ref[0]
    y = x * lax.rsqrt(jnp.mean(x * x, axis=-1, keepdims=True) + EPS) * g_ref[...]
    o_ref[0] = (y * (1.0 + sc_ref[0]) + sh_ref[0]).astype(BF16)


def _norm_mod(h, g, sc, sh):
    ts = CTX
    mod_spec = pl.BlockSpec((1, 1, D), lambda b, t: (jnp.where(t == 0, NB, b), 0, 0))
    return pl.pallas_call(
        _norm_mod_kernel,
        grid=(NB, S // ts),
        in_specs=[
            pl.BlockSpec((1, ts, D), lambda b, t: (b, t, 0)),
            pl.BlockSpec((1, D), lambda b, t: (0, 0)),
            mod_spec,
            mod_spec,
        ],
        out_specs=pl.BlockSpec((1, ts, D), lambda b, t: (b, t, 0)),
        out_shape=jax.ShapeDtypeStruct((NB, S, D), BF16),
        compiler_params=_params(("parallel", "parallel")),
        name="norm_mod",
    )(h, g.reshape(1, D), sc, sh)


def _cast_weight(w_ref, w_s):
    @pl.when(pl.program_id(1) == 0)
    def _():
        w_s[...] = w_ref[0].astype(BF16)


def _proj_call(kernel, x, w, layer, col_blk0, n_out, tn, extra, extra_specs, out_dtype, name, n_w=1,
               w_blk_offsets=(0,)):
    k = x.shape[-1]
    w_specs = [pl.BlockSpec((1, k, tn), functools.partial(lambda j, b, off: (layer, 0, col_blk0 + off + j), off=off))
               for off in w_blk_offsets]
    return pl.pallas_call(
        kernel,
        grid=(n_out // tn, NB),
        in_specs=[pl.BlockSpec((1, S, k), lambda j, b: (b, 0, 0))] + w_specs + extra_specs,
        out_specs=pl.BlockSpec((1, S, tn), lambda j, b: (b, 0, j)),
        out_shape=jax.ShapeDtypeStruct((NB, S, n_out), out_dtype),
        scratch_shapes=[pltpu.VMEM((k, tn), BF16) for _ in w_blk_offsets],
        compiler_params=_params(("parallel", "arbitrary")),
        name=name,
    )(x, *([w] * len(w_blk_offsets)), *extra)


def _ffn_up_kernel(x_ref, wg_ref, wv_ref, cw_ref, cb_ref, o_ref, wg_s, wv_s):
    _cast_weight(wg_ref, wg_s)
    _cast_weight(wv_ref, wv_s)
    x = x_ref[0]
    gate = _dot(x, wg_s[...])
    val = _dot(x, wv_s[...])
    cw = cw_ref[0]
    conv = cw[0:1] * _seg_shift(gate, -1) + cw[1:2] * gate + cw[2:3] * _seg_shift(gate, 1) + cb_ref[0]
    o_ref[0] = (_silu(conv) * val).astype(BF16)


def _ffn_up(n, w_up, conv_w, conv_b, layer):
    tn = 256
    nblk = D_FF // tn
    extra_specs = [
        pl.BlockSpec((1, FFN_CONV_K, tn), lambda j, b: (layer, 0, j)),
        pl.BlockSpec((1, 1, tn), lambda j, b: (layer, 0, j)),
    ]
    return _proj_call(_ffn_up_kernel, n, w_up, layer, 0, D_FF, tn,
                      (conv_w, conv_b.reshape(DEPTH, 1, D_FF)), extra_specs, BF16, "ffn_up",
                      w_blk_offsets=(0, nblk))


def _qk_kernel(x_ref, w_ref, g_ref, cos_ref, sa_ref, sb_ref, o_ref, w_s):
    _cast_weight(w_ref, w_s)
    y = _dot(x_ref[0], w_s[...])
    g = g_ref[0]
    cos, sa, sb = cos_ref[...], sa_ref[...], sb_ref[...]
    for hh in range(y.shape[1] // HEAD_DIM):
        t = y[:, hh * HEAD_DIM:(hh + 1) * HEAD_DIM]
        t = t * lax.rsqrt(jnp.mean(t * t, axis=-1, keepdims=True) + EPS) * g
        r = t * cos + pltpu.roll(t, 96, 1) * sa + pltpu.roll(t, 32, 1) * sb
        o_ref[0, :, hh * HEAD_DIM:(hh + 1) * HEAD_DIM] = r.astype(BF16)


def _plain_kernel(x_ref, w_ref, o_ref, w_s):
    _cast_weight(w_ref, w_s)
    o_ref[0] = _dot(x_ref[0], w_s[...]).astype(o_ref.dtype)


def _glu_kernel(x_ref, wa_ref, wb_ref, o_ref, wa_s, wb_s):
    _cast_weight(wa_ref, wa_s)
    _cast_weight(wb_ref, wb_s)
    x = x_ref[0]
    a = _dot(x, wa_s[...])
    b = _dot(x, wb_s[...])
    o_ref[0] = (a * jax.nn.sigmoid(b)).astype(BF16)


def _rope_tables():
    row = jnp.repeat(jnp.arange(SEQ // GRID_W, dtype=F32), GRID_W)
    col = jnp.tile(jnp.arange(GRID_W, dtype=F32), SEQ // GRID_W)
    half = HEAD_DIM // 2
    inv_freq = ROPE_THETA ** (-jnp.arange(0, half, 2, dtype=F32) / half)
    ang_r = row[:, None] * inv_freq
    ang_c = col[:, None] * inv_freq
    ang = jnp.concatenate([ang_r, ang_r, ang_c, ang_c], axis=-1)
    cos, sin = jnp.cos(ang), jnp.sin(ang)
    quarter = (jnp.arange(HEAD_DIM) // (HEAD_DIM // 4)) % 2
    sa = jnp.where(quarter == 0, -sin, 0.0)
    sb = jnp.where(quarter == 1, sin, 0.0)
    pad1 = jnp.ones((CTX, HEAD_DIM), F32)
    pad0 = jnp.zeros((CTX, HEAD_DIM), F32)
    return (jnp.concatenate([pad1, cos]), jnp.concatenate([pad0, sa]), jnp.concatenate([pad0, sb]))


def _even_proj(n, w_in, q_g, k_g, rope, i):
    tn = 256
    gains = jnp.stack([q_g, k_g]).reshape(2, 1, HEAD_DIM)
    tab_spec = pl.BlockSpec((S, HEAD_DIM), lambda j, b: (0, 0))
    qk = _proj_call(_qk_kernel, n, w_in, i, 0, A_Q_W + A_KV_W, tn, (gains,) + rope,
                    [pl.BlockSpec((1, 1, HEAD_DIM), lambda j, b: (jnp.where(j < A_Q_W // tn, 0, 1), 0, 0)),
                     tab_spec, tab_spec, tab_spec], BF16, "even_qk")
    v = _proj_call(_plain_kernel, n, w_in, i, (A_Q_W + A_KV_W) // tn, A_KV_W, tn, (), [], BF16, "even_v")
    glu0 = (A_Q_W + 2 * A_KV_W) // tn
    glu = _proj_call(_glu_kernel, n, w_in, i, glu0, CONV_CH, tn, (), [], BF16, "even_glu",
                     w_blk_offsets=(0, CONV_CH // tn))
    return qk, v, glu


def _mm_res_kernel(*refs, n_x, k_sizes, tm):
    x_refs = refs[:n_x]
    w_ref, h_ref, g_ref, o_ref, w_s = refs[n_x:]
    i = pl.program_id(1)
    _cast_weight(w_ref, w_s)
    acc = None
    off = 0
    for x_ref, kx in zip(x_refs, k_sizes):
        part = _dot(x_ref[...], w_s[off:off + kx, :])
        acc = part if acc is None else acc + part
        off += kx
    tiles_per_sample = S // tm
    b = i // tiles_per_sample
    row = lax.broadcasted_iota(jnp.int32, (tm, 1), 0) + (i % tiles_per_sample) * tm
    gate = jnp.where(row < CTX, g_ref[NB:NB + 1, :], g_ref[pl.ds(b, 1), :])
    o_ref[...] = h_ref[...] + gate * acc


def _mm_res(xs, w, layer, h, gate, name):
    tn, tm = 256, S // 2
    k_sizes = tuple(x.shape[-1] for x in xs)
    k = sum(k_sizes)
    m = NB * S
    kern = functools.partial(_mm_res_kernel, n_x=len(xs), k_sizes=k_sizes, tm=tm)
    return pl.pallas_call(
        kern,
        grid=(D // tn, m // tm),
        in_specs=[pl.BlockSpec((tm, kx), lambda j, i: (i, 0)) for kx in k_sizes] + [
            pl.BlockSpec((1, k, tn), lambda j, i: (layer, 0, j)),
            pl.BlockSpec((tm, tn), lambda j, i: (i, j)),
            pl.BlockSpec((8, tn), lambda j, i: (0, j)),
        ],
        out_specs=pl.BlockSpec((tm, tn), lambda j, i: (i, j)),
        out_shape=jax.ShapeDtypeStruct((m, D), F32),
        scratch_shapes=[pltpu.VMEM((k, tn), BF16)],
        compiler_params=_params(("parallel", "arbitrary")),
        name=name,
    )(*xs, w, h, gate)


_CPAD = 16
_CROWS = _CPAD + CTX + _CPAD + SEQ + _CPAD
_CBLK = 256


def _conformer_kernel(x_ref, w_ref, b_ref, g_ref, beta_ref, o_ref, pad_s):
    zeros = jnp.zeros((_CPAD, 128), F32)
    pad_s[0:_CPAD, :] = zeros
    pad_s[_CPAD + CTX:2 * _CPAD + CTX, :] = zeros
    pad_s[_CROWS - _CPAD:_CROWS, :] = zeros
    pad_s[_CPAD:_CPAD + CTX, :] = x_ref[0, 0:CTX, :].astype(F32)
    pad_s[2 * _CPAD + CTX:2 * _CPAD + S, :] = x_ref[0, CTX:S, :].astype(F32)
    w = w_ref[0]
    left = (CONV_K - 1) // 2
    for blk in range(S // _CBLK):
        r0 = blk * _CBLK
        p0 = r0 + (_CPAD if r0 < CTX else 2 * _CPAD)
        acc = jnp.zeros((_CBLK, 128), F32)
        for k in range(CONV_K):
            acc = acc + w[k:k + 1, :] * pad_s[p0 + k - left:p0 + k - left + _CBLK, :]
        hcv = acc + b_ref[0]
        mu = jnp.mean(hcv, axis=-1, keepdims=True)
        dlt = hcv - mu
        var = jnp.mean(dlt * dlt, axis=-1, keepdims=True)
        hn = dlt * lax.rsqrt(var + EPS) * g_ref[0] + beta_ref[0]
        o_ref[0, r0:r0 + _CBLK, :] = _silu(hn).astype(BF16)


def _conformer(glu, dw_w, dw_b, ln_g, ln_b, i):
    groups = CONV_CH // 128
    vec = lambda a: a.reshape(a.shape[0], 1, CONV_CH)
    vspec = pl.BlockSpec((1, 1, 128), lambda b, c: (i, 0, c))
    return pl.pallas_call(
        _conformer_kernel,
        grid=(NB, groups),
        in_specs=[
            pl.BlockSpec((1, S, 128), lambda b, c: (b, 0, c)),
            pl.BlockSpec((1, CONV_K, 128), lambda b, c: (i, 0, c)),
            vspec, vspec, vspec,
        ],
        out_specs=pl.BlockSpec((1, S, 128), lambda b, c: (b, 0, c)),
        out_shape=jax.ShapeDtypeStruct((NB, S, CONV_CH), BF16),
        scratch_shapes=[pltpu.VMEM((_CROWS, 128), F32)],
        compiler_params=_params(("parallel", "parallel")),
        name="conformer_conv",
    )(glu, dw_w, vec(dw_b), vec(ln_g), vec(ln_b))


def _attn_kernel(sink_ref, q_ref, kc_ref, vc_ref, *rest, band, layer_row):
    if band:
        kp_ref, kn_ref, kx_ref, vp_ref, vn_ref, vx_ref, o_ref = rest
    else:
        (o_ref,) = rest
    h = pl.program_id(1)
    nq = q_ref.shape[1]
    q = jnp.concatenate([q_ref[0, :, g * HEAD_DIM:(g + 1) * HEAD_DIM] for g in range(A_GROUP)], axis=0)
    scale = HEAD_DIM ** -0.5
    if band:
        n = pl.program_id(2)
        kcat = jnp.concatenate([kc_ref[0], kp_ref[0], kn_ref[0], kx_ref[0]], axis=0)
        vcat = jnp.concatenate([vc_ref[0], vp_ref[0], vn_ref[0], vx_ref[0]], axis=0)
    else:
        kcat, vcat = kc_ref[0], vc_ref[0]
    s = _dot_nt(q, kcat) * scale
    rows = A_GROUP * nq
    if band:
        nk = CTX + 3 * ABLK
        r = lax.broadcasted_iota(jnp.int32, (rows, nk), 0) % nq
        c = lax.broadcasted_iota(jnp.int32, (rows, nk), 1) - CTX
        rel = c - ABLK - r
        kpos = (n - 1) * ABLK + c
        ok = (c < 0) | ((jnp.abs(rel) <= WINDOW) & (kpos >= 0) & (kpos < SEQ))
        s = jnp.where(ok, s, -jnp.inf)
    gidx = lax.broadcasted_iota(jnp.int32, (rows, 1), 0) // nq
    sink = jnp.zeros((rows, 1), F32)
    for g in range(A_GROUP):
        sink = jnp.where(gidx == g, sink_ref[layer_row, h * A_GROUP + g], sink)
    m = jnp.maximum(jnp.max(s, axis=-1, keepdims=True), sink)
    p = jnp.exp(s - m)
    den = jnp.sum(p, axis=-1, keepdims=True) + jnp.exp(sink - m)
    o = _dot(p.astype(BF16), vcat) / den
    for g in range(A_GROUP):
        o_ref[0, :, g * HEAD_DIM:(g + 1) * HEAD_DIM] = o[g * nq:(g + 1) * nq].astype(BF16)


def _attention(qk, v, sink, i):
    gw = A_GROUP * HEAD_DIM
    kcol0 = A_Q_W // HEAD_DIM
    cb = CTX // ABLK
    nb = SEQ // ABLK
    smem = pl.BlockSpec(memory_space=pltpu.SMEM)

    def blk(rows, col_fn, row_fn):
        return pl.BlockSpec((1, rows, HEAD_DIM), lambda b, h, n: (b, row_fn(n), col_fn(h)))

    kcol = lambda h: kcol0 + h
    vcol = lambda h: h
    prev_r = lambda n: cb + jnp.maximum(n - 1, 0)
    cur_r = lambda n: cb + n
    next_r = lambda n: cb + jnp.minimum(n + 1, nb - 1)
    zero_r = lambda n: 0
    lat = pl.pallas_call(
        functools.partial(_attn_kernel, band=True, layer_row=i),
        grid=(NB, A_KV_HEADS, nb),
        in_specs=[
            smem,
            pl.BlockSpec((1, ABLK, gw), lambda b, h, n: (b, cb + n, h)),
            blk(CTX, kcol, zero_r), blk(CTX, vcol, zero_r),
            blk(ABLK, kcol, prev_r), blk(ABLK, kcol, cur_r), blk(ABLK, kcol, next_r),
            blk(ABLK, vcol, prev_r), blk(ABLK, vcol, cur_r), blk(ABLK, vcol, next_r),
        ],
        out_specs=pl.BlockSpec((1, ABLK, gw), lambda b, h, n: (b, n, h)),
        out_shape=jax.ShapeDtypeStruct((NB, SEQ, A_Q_W), BF16),
        compiler_params=_params(("parallel", "parallel", "parallel")),
        name="attn_latent",
    )(sink, qk, qk, v, qk, qk, qk, v, v, v)
    ctx = pl.pallas_call(
        functools.partial(_attn_kernel, band=False, layer_row=i),
        grid=(NB, A_KV_HEADS),
        in_specs=[
            smem,
            pl.BlockSpec((1, CTX, gw), lambda b, h: (b, 0, h)),
            pl.BlockSpec((1, CTX, HEAD_DIM), lambda b, h: (b, 0, kcol0 + h)),
            pl.BlockSpec((1, CTX, HEAD_DIM), lambda b, h: (b, 0, h)),
        ],
        out_specs=pl.BlockSpec((1, CTX, gw), lambda b, h: (b, 0, h)),
        out_shape=jax.ShapeDtypeStruct((NB, CTX, A_Q_W), BF16),
        compiler_params=_params(("parallel", "parallel")),
        name="attn_context",
    )(sink, qk, qk, v)
    return jnp.concatenate([ctx, lat], axis=1)


def _dn_conv_kernel(x_ref, w_ref, cw_ref, o_ref, w_s, *, tn):
    _cast_weight(w_ref, w_s)
    j = pl.program_id(0)
    y = _dot(x_ref[0], w_s[...])
    cw = cw_ref[0]
    left = (DN_SHORT_K - 1) // 2
    conv = cw[left:left + 1] * y
    for k in range(DN_SHORT_K):
        if k != left:
            conv = conv + cw[k:k + 1] * _seg_shift(y, k - left)
    a = _silu(conv)
    is_qk = j < (2 * DN_QK_W) // tn
    qscale = jnp.where(j < DN_QK_W // tn, DN_DK ** -0.5, 1.0)
    for hh in range(tn // DN_DK):
        t = a[:, hh * DN_DK:(hh + 1) * DN_DK]
        inv = lax.rsqrt(jnp.sum(t * t, axis=-1, keepdims=True) + EPS) * qscale
        t = t * jnp.where(is_qk, inv, 1.0)
        o_ref[0, :, hh * DN_DK:(hh + 1) * DN_DK] = t.astype(BF16)


def _dn_gate_kernel(x_ref, w_ref, alog_ref, dtb_ref, o_ref, w_s):
    @pl.when((pl.program_id(0) == 0) & (pl.program_id(1) == 0))
    def _():
        w_s[...] = w_ref[0].astype(BF16)

    ba = _dot(x_ref[0], w_s[...])
    beta = jax.nn.sigmoid(ba)
    g = -jnp.exp(alog_ref[...]) * jax.nn.softplus(ba + dtb_ref[...])
    ii = lax.broadcasted_iota(jnp.int32, (CHUNK, CHUNK), 0)
    jj = lax.broadcasted_iota(jnp.int32, (CHUNK, CHUNK), 1)
    tri_f = jnp.where(ii >= jj, 1.0, 0.0).astype(BF16)
    tri_b = jnp.where(ii <= jj, 1.0, 0.0).astype(BF16)
    g1 = g.astype(BF16)
    r1 = g - g1.astype(F32)
    g2 = r1.astype(BF16)
    g3 = (r1 - g2.astype(F32)).astype(BF16)
    cs_f = _dot(tri_f, g1) + _dot(tri_f, g2) + _dot(tri_f, g3)
    cs_b = _dot(tri_b, g1) + _dot(tri_b, g2) + _dot(tri_b, g3)
    col = lax.broadcasted_iota(jnp.int32, (CHUNK, 128), 1)
    o_ref[0] = jnp.where(col < 2 * DN_V_HEADS, beta, jnp.where(col < 3 * DN_V_HEADS, cs_f, cs_b))


def _dn_project(n, w_in, conv_w, a_log, dt_bias, i):
    tn = 256
    qkv = _proj_call(functools.partial(_dn_conv_kernel, tn=tn), n, w_in, i, 0, DN_CONV_CH, tn, (conv_w,),
                     [pl.BlockSpec((1, DN_SHORT_K, tn), lambda j, b: (i, 0, j))], BF16, "dn_qkv")
    z = _proj_call(_plain_kernel, n, w_in, i, DN_CONV_CH // tn, DN_V_W, tn, (), [], BF16, "dn_z")
    zeros = jnp.zeros((2 * DN_V_HEADS,), F32)
    alog = jnp.concatenate([zeros, a_log[i].reshape(-1)]).reshape(1, 128)
    dtb = jnp.concatenate([zeros, dt_bias[i].reshape(-1)]).reshape(1, 128)
    gates = pl.pallas_call(
        _dn_gate_kernel,
        grid=(NB, NCH),
        in_specs=[
            pl.BlockSpec((1, CHUNK, D), lambda b, c: (b, c, 0)),
            pl.BlockSpec((1, D, 128), lambda b, c: (i, 0, (DN_CONV_CH + DN_V_W) // 128)),
            pl.BlockSpec((1, 128), lambda b, c: (0, 0)),
            pl.BlockSpec((1, 128), lambda b, c: (0, 0)),
        ],
        out_specs=pl.BlockSpec((1, CHUNK, 128), lambda b, c: (b, c, 0)),
        out_shape=jax.ShapeDtypeStruct((NB, S, 128), F32),
        scratch_shapes=[pltpu.VMEM((D, 128), BF16)],
        compiler_params=_params(("arbitrary", "arbitrary")),
        name="dn_gates",
    )(n, w_in, alog, dtb)
    return qkv, z, gates


_DN_G = 16
_DN_HG = DN_V_HEADS // _DN_G
_DN_REP = DN_V_HEADS // DN_K_HEADS


def _bmm(a, b):
    return lax.dot_general(a, b, (((2,), (1,)), ((0,), (0,))), preferred_element_type=F32)


def _bmm_nt(a, b):
    return lax.dot_general(a, b, (((2,), (2,)), ((0,), (0,))), preferred_element_type=F32)


def _bmm_tn(a, b):
    return lax.dot_general(a, b, (((1,), (1,)), ((0,), (0,))), preferred_element_type=F32)


def _hi_lo(x):
    hi = x.astype(BF16).astype(F32)
    return hi, x - hi


def _bmm_f32(a, b):
    ah, al = _hi_lo(a)
    bh, bl = _hi_lo(b)
    lhs = jnp.concatenate([ah, al, ah], axis=2).astype(BF16)
    rhs = jnp.concatenate([bh, bh, bl], axis=1).astype(BF16)
    return _bmm(lhs, rhs)


def _bmm_f32_lhs(a, b):
    ah, al = _hi_lo(a)
    lhs = jnp.concatenate([ah, al], axis=2).astype(BF16)
    return _bmm(lhs, jnp.concatenate([b, b], axis=1))


def _dn_scan_kernel(q_ref, k_ref, v_ref, g1_ref, g2_ref, o_ref, s_ref):
    d = pl.program_id(1)
    step = pl.program_id(3)
    ng, nk = _DN_G, _DN_G // _DN_REP

    @pl.when(step == 0)
    def _():
        s_ref[...] = jnp.zeros_like(s_ref)

    fwd = d == 0
    ii = lax.broadcasted_iota(jnp.int32, (CHUNK, CHUNK), 0)
    jj = lax.broadcasted_iota(jnp.int32, (CHUNK, CHUNK), 1)
    later = (ii - jj) * jnp.where(fwd, 1, -1)
    incl = later >= 0
    strict = jnp.where(later > 0, 1.0, 0.0)
    eye = jnp.where(ii == jj, 1.0, 0.0)

    def same_block(size):
        return jnp.where((ii // size) == (jj // size), 1.0, 0.0)

    g1 = g1_ref[0, 0, 0]
    g2 = g2_ref[0, 0, 0, 0]
    q3 = jnp.stack([q_ref[0, :, i * DN_DK:(i + 1) * DN_DK] for i in range(nk)])
    k3 = jnp.stack([k_ref[0, :, i * DN_DK:(i + 1) * DN_DK] for i in range(nk)])
    v3 = jnp.stack([v_ref[0, :, e * DN_DV:(e + 1) * DN_DV] for e in range(ng)])
    beta_r = jnp.stack([g2[e:e + 1, :] for e in range(ng)])
    gc_r = jnp.stack([g2[ng + e:ng + e + 1, :] for e in range(ng)])
    beta_c = jnp.stack([jnp.broadcast_to(g1[:, e:e + 1], (CHUNK, CHUNK)) for e in range(ng)])
    gc_c = jnp.stack([jnp.broadcast_to(g1[:, ng + e:ng + e + 1], (CHUNK, DN_DK)) for e in range(ng)])
    g_last = jnp.where(fwd, gc_r[:, :, CHUNK - 1:CHUNK], gc_r[:, :, 0:1])

    decay = jnp.exp(jnp.where(incl, gc_c[:, :, :CHUNK] - gc_r, -jnp.inf))
    kk = jnp.repeat(_bmm_nt(k3, k3), _DN_REP, axis=0)
    qk = jnp.repeat(_bmm_nt(q3, k3), _DN_REP, axis=0)
    a = kk * (beta_c * decay) * strict

    d4 = a * same_block(4)
    imd = eye - d4
    t = imd + _bmm_f32(imd, _bmm_f32(d4, d4))
    size = 4
    while size < CHUNK:
        n_s = a * (same_block(2 * size) - same_block(size))
        t = t - _bmm_f32(t, _bmm_f32(n_s, t))
        size *= 2

    k_v = jnp.repeat(k3, _DN_REP, axis=0)
    tb = t * beta_r
    u = _bmm_f32_lhs(tb, v3)
    w = _bmm_f32_lhs(tb * jnp.exp(gc_r), k_v)

    st = s_ref[...]
    stb = st.astype(BF16)
    v_new = u - _bmm(w.astype(BF16), stb)
    vnb = v_new.astype(BF16)
    qg = (jnp.repeat(q3, _DN_REP, axis=0).astype(F32) * jnp.exp(gc_c)).astype(BF16)
    o = _bmm(qg, stb) + _bmm((qk * decay).astype(BF16), vnb)
    for e in range(ng):
        o_ref[0, 0, :, e * DN_DV:(e + 1) * DN_DV] = o[e]
    kd = (k_v.astype(F32) * jnp.exp(g_last - gc_c)).astype(BF16)
    s_ref[...] = st * jnp.exp(g_last) + _bmm_tn(kd, vnb)


def _dn_scan(qkv, gates):
    hpg = DN_V_HEADS // _DN_HG
    kw = (DN_K_HEADS // _DN_HG) * DN_DK
    vw = hpg * DN_DV
    bt = gates[..., :2 * DN_V_HEADS].reshape(NB, S, 2, _DN_HG, hpg)
    gc = gates[..., 2 * DN_V_HEADS:].reshape(NB, S, 2, _DN_HG, hpg)
    g1 = jnp.transpose(jnp.concatenate([bt, gc], axis=-1), (0, 2, 3, 1, 4))
    g2 = jnp.swapaxes(g1.reshape(NB, 2, _DN_HG, NCH, CHUNK, 2 * hpg), -1, -2)

    def chunk(d, s):
        return jnp.where(d == 0, s, jnp.where(s < CTX_CH, CTX_CH - 1 - s, NCH + CTX_CH - 1 - s))

    return pl.pallas_call(
        _dn_scan_kernel,
        grid=(NB, 2, _DN_HG, NCH),
        in_specs=[
            pl.BlockSpec((1, CHUNK, kw), lambda b, d, g, s: (b, chunk(d, s), g)),
            pl.BlockSpec((1, CHUNK, kw), lambda b, d, g, s: (b, chunk(d, s), DN_QK_W // kw + g)),
            pl.BlockSpec((1, CHUNK, vw), lambda b, d, g, s: (b, chunk(d, s), 2 * DN_QK_W // vw + g)),
            pl.BlockSpec((1, 1, 1, CHUNK, 2 * hpg), lambda b, d, g, s: (b, d, g, chunk(d, s), 0)),
            pl.BlockSpec((1, 1, 1, 1, 2 * hpg, CHUNK), lambda b, d, g, s: (b, d, g, chunk(d, s), 0, 0)),
        ],
        out_specs=pl.BlockSpec((1, 1, CHUNK, vw), lambda b, d, g, s: (b, d, chunk(d, s), g)),
        out_shape=jax.ShapeDtypeStruct((NB, 2, S, DN_V_W), F32),
        scratch_shapes=[pltpu.VMEM((hpg, DN_DK, DN_DV), F32)],
        compiler_params=_params(("parallel", "parallel", "parallel", "arbitrary")),
        name="dn_scan",
    )(qkv, qkv, qkv, g1, g2)


def _dn_out_kernel(o_ref, z_ref, g_ref, y_ref):
    g = g_ref[...]
    for hh in range(o_ref.shape[-1] // DN_DV):
        sl = slice(hh * DN_DV, (hh + 1) * DN_DV)
        o = o_ref[0, 0, :, sl] + o_ref[0, 1, :, sl]
        o = o * lax.rsqrt(jnp.mean(o * o, axis=-1, keepdims=True) + EPS) * g
        y_ref[0, :, sl] = (o * _silu(z_ref[0, :, sl].astype(F32))).astype(BF16)


def _dn_gated_norm(o, z, norm_g):
    ts, tw = 256, 1024
    return pl.pallas_call(
        _dn_out_kernel,
        grid=(NB, S // ts, DN_V_W // tw),
        in_specs=[
            pl.BlockSpec((1, 2, ts, tw), lambda b, t, c: (b, 0, t, c)),
            pl.BlockSpec((1, ts, tw), lambda b, t, c: (b, t, c)),
            pl.BlockSpec((1, DN_DV), lambda b, t, c: (0, 0)),
        ],
        out_specs=pl.BlockSpec((1, ts, tw), lambda b, t, c: (b, t, c)),
        out_shape=jax.ShapeDtypeStruct((NB, S, DN_V_W), BF16),
        compiler_params=_params(("parallel", "parallel", "parallel")),
        name="dn_gated_norm",
    )(o, z, norm_g.reshape(1, DN_DV))


def kernel(x, c, ctx, c_ctx, w_mod, b_mod, norm1_g, norm2_g, ffn_w_up, ffn_conv_w, ffn_conv_b, ffn_w_down, even_w_in, even_w_out, attn_q_norm_g, attn_k_norm_g, attn_sink, conv_dw_w, conv_dw_b, conv_ln_g, conv_ln_b, dn_w_in, dn_conv_w, dn_a_log, dn_dt_bias, dn_norm_g, dn_w_out):
    cvec = jnp.concatenate([c, c_ctx[None, :], jnp.zeros((8 - NB - 1, D), F32)], axis=0)
    mods = _modulation(cvec, w_mod, b_mod).reshape(DEPTH, 8, N_MOD, D)
    rope = _rope_tables()
    h = jnp.concatenate([ctx, x], axis=1)
    m = NB * S
    for layer in range(DEPTH):
        i = layer // 2
        sh1, sc1, g1, sh2, sc2, g2 = [mods[layer, :, j, :] for j in range(N_MOD)]
        n = _norm_mod(h, norm1_g[layer], sc1.reshape(8, 1, D), sh1.reshape(8, 1, D))
        hf = h.reshape(m, D)
        if layer % 2 == 0:
            qk, v, glu = _even_proj(n, even_w_in, attn_q_norm_g[i], attn_k_norm_g[i], rope, i)
            att = _attention(qk, v, attn_sink, i)
            cv = _conformer(glu, conv_dw_w, conv_dw_b, conv_ln_g, conv_ln_b, i)
            hf = _mm_res([att.reshape(m, A_Q_W), cv.reshape(m, CONV_CH)], even_w_out, i, hf, g1, "even_out")
        else:
            qkv, z, gates = _dn_project(n, dn_w_in, dn_conv_w, dn_a_log, dn_dt_bias, i)
            o = _dn_scan(qkv, gates)
            y = _dn_gated_norm(o, z, dn_norm_g[i])
            hf = _mm_res([y.reshape(m, DN_V_W)], dn_w_out, i, hf, g1, "dn_out")
        h = hf.reshape(NB, S, D)
        n2 = _norm_mod(h, norm2_g[layer], sc2.reshape(8, 1, D), sh2.reshape(8, 1, D))
        f = _ffn_up(n2, ffn_w_up, ffn_conv_w, ffn_conv_b, layer)
        h = _mm_res([f.reshape(m, D_FF)], ffn_w_down, layer, h.reshape(m, D), g2, "ffn_down").reshape(NB, S, D)
    return h[:, CTX:, :]
```

```python
import functools
import math

import jax
import jax.numpy as jnp
from jax import lax
from jax.experimental import pallas as pl
from jax.experimental.pallas import tpu as pltpu

F32 = jnp.float32
BF16 = jnp.bfloat16

D = 2048
NB = 4
SEQ = 2048
CTX = 256
S = CTX + SEQ
DEPTH = 4
GRID_W = 64
EPS = 1e-6
N_MOD = 6

HEAD_DIM = 128
A_Q_HEADS = 8
A_KV_HEADS = 2
A_GROUP = 4
A_Q_W = 1024
A_KV_W = 256
WINDOW = 128
ABLK = 128
ROPE_THETA = 10000.0
CONV_CH = 1024
CONV_K = 31
EVEN_IN_W = 3584

DN_DK = 128
DN_DV = 128
DN_K_HEADS = 16
DN_V_HEADS = 32
DN_QK_W = 2048
DN_V_W = 4096
DN_CONV_CH = 8192
DN_BA_W = 128
DN_SHORT_K = 5
CHUNK = 64
NCH = S // CHUNK
CTX_CH = CTX // CHUNK

D_FF = 5632
FFN_CONV_K = 3

VMEM_LIMIT = 58 * 1024 * 1024


def _params(sem, vmem=VMEM_LIMIT):
    return pltpu.CompilerParams(dimension_semantics=sem, vmem_limit_bytes=vmem)


def _silu(x):
    return x * jax.nn.sigmoid(x)


def _dot(a, b):
    return jnp.dot(a, b, preferred_element_type=F32)


def _dot_nt(a, b):
    return lax.dot_general(a, b, (((1,), (1,)), ((), ())), preferred_element_type=F32)


def _dot_tn(a, b):
    return lax.dot_general(a, b, (((0,), (0,)), ((), ())), preferred_element_type=F32)


_HALO = 8
_PAD_ROWS = 3 * _HALO + S


def _seg_conv(y, pad_s, cw):
    taps = cw.shape[0]
    left = (taps - 1) // 2
    zeros = jnp.zeros((_HALO, y.shape[1]), F32)
    c0, l0 = _HALO, 2 * _HALO + CTX
    pad_s[0:_HALO, :] = zeros
    pad_s[c0 + CTX:l0, :] = zeros
    pad_s[l0 + SEQ:_PAD_ROWS, :] = zeros
    pad_s[c0:c0 + CTX, :] = y[:CTX]
    pad_s[l0:l0 + SEQ, :] = y[CTX:]

    def seg(p0, rows):
        acc = None
        for k in range(taps):
            term = cw[k:k + 1] * pad_s[p0 + k - left:p0 + k - left + rows, :]
            acc = term if acc is None else acc + term
        return acc

    return jnp.concatenate([seg(c0, CTX), seg(l0, SEQ)], axis=0)


def _mod_kernel(c_ref, w_ref, b_ref, o_ref):
    a = _silu(c_ref[...]).astype(BF16)
    o_ref[0] = _dot(a, w_ref[0].astype(BF16)) + b_ref[0]


def _modulation(cvec, w_mod, b_mod):
    tn = 1024
    n = N_MOD * D
    return pl.pallas_call(
        _mod_kernel,
        grid=(DEPTH, n // tn),
        in_specs=[
            pl.BlockSpec((8, D), lambda l, j: (0, 0)),
            pl.BlockSpec((1, D, tn), lambda l, j: (l, 0, j)),
            pl.BlockSpec((1, 1, tn), lambda l, j: (l, 0, j)),
        ],
        out_specs=pl.BlockSpec((1, 8, tn), lambda l, j: (l, 0, j)),
        out_shape=jax.ShapeDtypeStruct((DEPTH, 8, n), F32),
        compiler_params=_params(("parallel", "parallel")),
        name="adaln_mod",
    )(cvec, w_mod, b_mod.reshape(DEPTH, 1, n))


def _norm_mod_kernel(h_ref, g_ref, sc_ref, sh_ref, o_ref):
    x = h_ref[0]
    y = x * lax.rsqrt(jnp.mean(x * x, axis=-1, keepdims=True) + EPS) * g_ref[...]
    o_ref[0] = (y * (1.0 + sc_ref[0]) + sh_ref[0]).astype(BF16)


def _norm_mod(h, g, sc, sh):
    ts = CTX
    mod_spec = pl.BlockSpec((1, 1, D), lambda b, t: (jnp.where(t == 0, NB, b), 0, 0))
    return pl.pallas_call(
        _norm_mod_kernel,
        grid=(NB, S // ts),
        in_specs=[
            pl.BlockSpec((1, ts, D), lambda b, t: (b, t, 0)),
            pl.BlockSpec((1, D), lambda b, t: (0, 0)),
            mod_spec,
            mod_spec,
        ],
        out_specs=pl.BlockSpec((1, ts, D), lambda b, t: (b, t, 0)),
        out_shape=jax.ShapeDtypeStruct((NB, S, D), BF16),
        compiler_params=_params(("parallel", "parallel")),
        name="norm_mod",
    )(h, g.reshape(1, D), sc, sh)


def _xw(x_ref, w_ref):
    return _dot(x_ref[0], w_ref[0].astype(BF16))


def _proj_call(kernel, x, w, layer, col_blk0, n_out, tn, extra, extra_specs, out_dtype, name,
               w_blk_offsets=(0,), scratch=()):
    k = x.shape[-1]
    w_specs = [pl.BlockSpec((1, k, tn), functools.partial(lambda b, j, off: (layer, 0, col_blk0 + off + j), off=off))
               for off in w_blk_offsets]
    return pl.pallas_call(
        kernel,
        grid=(NB, n_out // tn),
        in_specs=[pl.BlockSpec((1, S, k), lambda b, j: (b, 0, 0))] + w_specs + extra_specs,
        out_specs=pl.BlockSpec((1, S, tn), lambda b, j: (b, 0, j)),
        out_shape=jax.ShapeDtypeStruct((NB, S, n_out), out_dtype),
        scratch_shapes=list(scratch),
        compiler_params=_params(("parallel", "parallel")),
        name=name,
    )(x, *([w] * len(w_blk_offsets)), *extra)


def _ffn_up_kernel(x_ref, wg_ref, wv_ref, cw_ref, cb_ref, o_ref, pad_s):
    gate = _xw(x_ref, wg_ref)
    val = _xw(x_ref, wv_ref)
    conv = _seg_conv(gate, pad_s, cw_ref[0]) + cb_ref[0]
    o_ref[0] = (_silu(conv) * val).astype(BF16)


def _ffn_up(n, w_up, conv_w, conv_b, layer):
    tn = 256
    nblk = D_FF // tn
    extra_specs = [
        pl.BlockSpec((1, FFN_CONV_K, tn), lambda b, j: (layer, 0, j)),
        pl.BlockSpec((1, 1, tn), lambda b, j: (layer, 0, j)),
    ]
    return _proj_call(_ffn_up_kernel, n, w_up, layer, 0, D_FF, tn,
                      (conv_w, conv_b.reshape(DEPTH, 1, D_FF)), extra_specs, BF16, "ffn_up",
                      w_blk_offsets=(0, nblk), scratch=[pltpu.VMEM((_PAD_ROWS, tn), F32)])


def _qk_kernel(x_ref, w_ref, g_ref, cos_ref, sa_ref, sb_ref, o_ref):
    y = _xw(x_ref, w_ref)
    g = g_ref[0]
    cos, sa, sb = cos_ref[...], sa_ref[...], sb_ref[...]
    for hh in range(y.shape[1] // HEAD_DIM):
        t = y[:, hh * HEAD_DIM:(hh + 1) * HEAD_DIM]
        t = t * lax.rsqrt(jnp.mean(t * t, axis=-1, keepdims=True) + EPS) * g
        r = t * cos + pltpu.roll(t, 96, 1) * sa + pltpu.roll(t, 32, 1) * sb
        o_ref[0, :, hh * HEAD_DIM:(hh + 1) * HEAD_DIM] = r.astype(BF16)


def _plain_kernel(x_ref, w_ref, o_ref):
    o_ref[0] = _xw(x_ref, w_ref).astype(o_ref.dtype)


def _glu_kernel(x_ref, wa_ref, wb_ref, o_ref):
    a = _xw(x_ref, wa_ref)
    b = _xw(x_ref, wb_ref)
    o_ref[0] = (a * jax.nn.sigmoid(b)).astype(BF16)


def _rope_tables():
    row = jnp.repeat(jnp.arange(SEQ // GRID_W, dtype=F32), GRID_W)
    col = jnp.tile(jnp.arange(GRID_W, dtype=F32), SEQ // GRID_W)
    half = HEAD_DIM // 2
    inv_freq = ROPE_THETA ** (-jnp.arange(0, half, 2, dtype=F32) / half)
    ang_r = row[:, None] * inv_freq
    ang_c = col[:, None] * inv_freq
    ang = jnp.concatenate([ang_r, ang_r, ang_c, ang_c], axis=-1)
    cos, sin = jnp.cos(ang), jnp.sin(ang)
    quarter = (jnp.arange(HEAD_DIM) // (HEAD_DIM // 4)) % 2
    sa = jnp.where(quarter == 0, -sin, 0.0)
    sb = jnp.where(quarter == 1, sin, 0.0)
    pad1 = jnp.ones((CTX, HEAD_DIM), F32)
    pad0 = jnp.zeros((CTX, HEAD_DIM), F32)
    return (jnp.concatenate([pad1, cos]), jnp.concatenate([pad0, sa]), jnp.concatenate([pad0, sb]))


def _even_proj(n, w_in, q_g, k_g, rope, i):
    tn = 256
    gains = jnp.stack([q_g, k_g]).reshape(2, 1, HEAD_DIM)
    tab_spec = pl.BlockSpec((S, HEAD_DIM), lambda b, j: (0, 0))
    qk = _proj_call(_qk_kernel, n, w_in, i, 0, A_Q_W + A_KV_W, tn, (gains,) + rope,
                    [pl.BlockSpec((1, 1, HEAD_DIM), lambda b, j: (jnp.where(j < A_Q_W // tn, 0, 1), 0, 0)),
                     tab_spec, tab_spec, tab_spec], BF16, "even_qk")
    v = _proj_call(_plain_kernel, n, w_in, i, (A_Q_W + A_KV_W) // tn, A_KV_W, tn, (), [], BF16, "even_v")
    glu0 = (A_Q_W + 2 * A_KV_W) // tn
    glu = _proj_call(_glu_kernel, n, w_in, i, glu0, CONV_CH, tn, (), [], BF16, "even_glu",
                     w_blk_offsets=(0, CONV_CH // tn))
    return qk, v, glu


def _mm_res_kernel(*refs, n_x, k_sizes, tm):
    x_refs = refs[:n_x]
    w_ref, h_ref, g_ref, o_ref = refs[n_x:]
    i = pl.program_id(0)
    acc = None
    off = 0
    for x_ref, kx in zip(x_refs, k_sizes):
        part = _dot(x_ref[...], w_ref[0, off:off + kx, :].astype(BF16))
        acc = part if acc is None else acc + part
        off += kx
    tiles_per_sample = S // tm
    b = i // tiles_per_sample
    row = lax.broadcasted_iota(jnp.int32, (tm, 1), 0) + (i % tiles_per_sample) * tm
    gate = jnp.where(row < CTX, g_ref[NB:NB + 1, :], g_ref[pl.ds(b, 1), :])
    o_ref[...] = h_ref[...] + gate * acc


def _mm_res(xs, w, layer, h, gate, name):
    tn, tm = 256, S // 2
    k_sizes = tuple(x.shape[-1] for x in xs)
    k = sum(k_sizes)
    m = NB * S
    kern = functools.partial(_mm_res_kernel, n_x=len(xs), k_sizes=k_sizes, tm=tm)
    return pl.pallas_call(
        kern,
        grid=(m // tm, D // tn),
        in_specs=[pl.BlockSpec((tm, kx), lambda i, j: (i, 0)) for kx in k_sizes] + [
            pl.BlockSpec((1, k, tn), lambda i, j: (layer, 0, j)),
            pl.BlockSpec((tm, tn), lambda i, j: (i, j)),
            pl.BlockSpec((8, tn), lambda i, j: (0, j)),
        ],
        out_specs=pl.BlockSpec((tm, tn), lambda i, j: (i, j)),
        out_shape=jax.ShapeDtypeStruct((m, D), F32),
        compiler_params=_params(("parallel", "parallel")),
        name=name,
    )(*xs, w, h, gate)


_CPAD = 16
_CROWS = _CPAD + CTX + _CPAD + SEQ + _CPAD
_CBLK = 256


def _conformer_kernel(x_ref, w_ref, b_ref, g_ref, beta_ref, o_ref, pad_s):
    zeros = jnp.zeros((_CPAD, 128), F32)
    pad_s[0:_CPAD, :] = zeros
    pad_s[_CPAD + CTX:2 * _CPAD + CTX, :] = zeros
    pad_s[_CROWS - _CPAD:_CROWS, :] = zeros
    pad_s[_CPAD:_CPAD + CTX, :] = x_ref[0, 0:CTX, :].astype(F32)
    pad_s[2 * _CPAD + CTX:2 * _CPAD + S, :] = x_ref[0, CTX:S, :].astype(F32)
    w = w_ref[0]
    left = (CONV_K - 1) // 2
    for blk in range(S // _CBLK):
        r0 = blk * _CBLK
        p0 = r0 + (_CPAD if r0 < CTX else 2 * _CPAD)
        acc = jnp.zeros((_CBLK, 128), F32)
        for k in range(CONV_K):
            acc = acc + w[k:k + 1, :] * pad_s[p0 + k - left:p0 + k - left + _CBLK, :]
        hcv = acc + b_ref[0]
        mu = jnp.mean(hcv, axis=-1, keepdims=True)
        dlt = hcv - mu
        var = jnp.mean(dlt * dlt, axis=-1, keepdims=True)
        hn = dlt * lax.rsqrt(var + EPS) * g_ref[0] + beta_ref[0]
        o_ref[0, r0:r0 + _CBLK, :] = _silu(hn).astype(BF16)


def _conformer(glu, dw_w, dw_b, ln_g, ln_b, i):
    groups = CONV_CH // 128
    vec = lambda a: a.reshape(a.shape[0], 1, CONV_CH)
    vspec = pl.BlockSpec((1, 1, 128), lambda b, c: (i, 0, c))
    return pl.pallas_call(
        _conformer_kernel,
        grid=(NB, groups),
        in_specs=[
            pl.BlockSpec((1, S, 128), lambda b, c: (b, 0, c)),
            pl.BlockSpec((1, CONV_K, 128), lambda b, c: (i, 0, c)),
            vspec, vspec, vspec,
        ],
        out_specs=pl.BlockSpec((1, S, 128), lambda b, c: (b, 0, c)),
        out_shape=jax.ShapeDtypeStruct((NB, S, CONV_CH), BF16),
        scratch_shapes=[pltpu.VMEM((_CROWS, 128), F32)],
        compiler_params=_params(("parallel", "parallel")),
        name="conformer_conv",
    )(glu, dw_w, vec(dw_b), vec(ln_g), vec(ln_b))


def _attn_kernel(sink_ref, q_ref, kc_ref, vc_ref, *rest, band, layer_row):
    if band:
        kp_ref, kn_ref, kx_ref, vp_ref, vn_ref, vx_ref, o_ref = rest
    else:
        (o_ref,) = rest
    h = pl.program_id(1)
    nq = q_ref.shape[1]
    q = jnp.concatenate([q_ref[0, :, g * HEAD_DIM:(g + 1) * HEAD_DIM] for g in range(A_GROUP)], axis=0)
    scale = HEAD_DIM ** -0.5
    if band:
        n = pl.program_id(2)
        kcat = jnp.concatenate([kc_ref[0], kp_ref[0], kn_ref[0], kx_ref[0]], axis=0)
        vcat = jnp.concatenate([vc_ref[0], vp_ref[0], vn_ref[0], vx_ref[0]], axis=0)
    else:
        kcat, vcat = kc_ref[0], vc_ref[0]
    s = _dot_nt(q, kcat) * scale
    rows = A_GROUP * nq
    if band:
        nk = CTX + 3 * ABLK
        r = lax.broadcasted_iota(jnp.int32, (rows, nk), 0) % nq
        c = lax.broadcasted_iota(jnp.int32, (rows, nk), 1) - CTX
        rel = c - ABLK - r
        kpos = (n - 1) * ABLK + c
        ok = (c < 0) | ((jnp.abs(rel) <= WINDOW) & (kpos >= 0) & (kpos < SEQ))
        s = jnp.where(ok, s, -jnp.inf)
    gidx = lax.broadcasted_iota(jnp.int32, (rows, 1), 0) // nq
    sink = jnp.zeros((rows, 1), F32)
    for g in range(A_GROUP):
        sink = jnp.where(gidx == g, sink_ref[layer_row, h * A_GROUP + g], sink)
    m = jnp.maximum(jnp.max(s, axis=-1, keepdims=True), sink)
    p = jnp.exp(s - m)
    den = jnp.sum(p, axis=-1, keepdims=True) + jnp.exp(sink - m)
    o = _dot(p.astype(BF16), vcat) / den
    for g in range(A_GROUP):
        o_ref[0, :, g * HEAD_DIM:(g + 1) * HEAD_DIM] = o[g * nq:(g + 1) * nq].astype(BF16)


def _attention(qk, v, sink, i):
    gw = A_GROUP * HEAD_DIM
    kcol0 = A_Q_W // HEAD_DIM
    cb = CTX // ABLK
    nb = SEQ // ABLK
    smem = pl.BlockSpec(memory_space=pltpu.SMEM)

    def blk(rows, col_fn, row_fn):
        return pl.BlockSpec((1, rows, HEAD_DIM), lambda b, h, n: (b, row_fn(n), col_fn(h)))

    kcol = lambda h: kcol0 + h
    vcol = lambda h: h
    prev_r = lambda n: cb + jnp.maximum(n - 1, 0)
    cur_r = lambda n: cb + n
    next_r = lambda n: cb + jnp.minimum(n + 1, nb - 1)
    zero_r = lambda n: 0
    lat = pl.pallas_call(
        functools.partial(_attn_kernel, band=True, layer_row=i),
        grid=(NB, A_KV_HEADS, nb),
        in_specs=[
            smem,
            pl.BlockSpec((1, ABLK, gw), lambda b, h, n: (b, cb + n, h)),
            blk(CTX, kcol, zero_r), blk(CTX, vcol, zero_r),
            blk(ABLK, kcol, prev_r), blk(ABLK, kcol, cur_r), blk(ABLK, kcol, next_r),
            blk(ABLK, vcol, prev_r), blk(ABLK, vcol, cur_r), blk(ABLK, vcol, next_r),
        ],
        out_specs=pl.BlockSpec((1, ABLK, gw), lambda b, h, n: (b, n, h)),
        out_shape=jax.ShapeDtypeStruct((NB, SEQ, A_Q_W), BF16),
        compiler_params=_params(("parallel", "parallel", "parallel")),
        name="attn_latent",
    )(sink, qk, qk, v, qk, qk, qk, v, v, v)
    ctx = pl.pallas_call(
        functools.partial(_attn_kernel, band=False, layer_row=i),
        grid=(NB, A_KV_HEADS),
        in_specs=[
            smem,
            pl.BlockSpec((1, CTX, gw), lambda b, h: (b, 0, h)),
            pl.BlockSpec((1, CTX, HEAD_DIM), lambda b, h: (b, 0, kcol0 + h)),
            pl.BlockSpec((1, CTX, HEAD_DIM), lambda b, h: (b, 0, h)),
        ],
        out_specs=pl.BlockSpec((1, CTX, gw), lambda b, h: (b, 0, h)),
        out_shape=jax.ShapeDtypeStruct((NB, CTX, A_Q_W), BF16),
        compiler_params=_params(("parallel", "parallel")),
        name="attn_context",
    )(sink, qk, qk, v)
    return jnp.concatenate([ctx, lat], axis=1)


def _dn_conv_kernel(x_ref, w_ref, cw_ref, o_ref, pad_s, *, tn):
    j = pl.program_id(1)
    a = _silu(_seg_conv(_xw(x_ref, w_ref), pad_s, cw_ref[0]))
    is_qk = j < (2 * DN_QK_W) // tn
    qscale = jnp.where(j < DN_QK_W // tn, DN_DK ** -0.5, 1.0)
    for hh in range(tn // DN_DK):
        t = a[:, hh * DN_DK:(hh + 1) * DN_DK]
        inv = lax.rsqrt(jnp.sum(t * t, axis=-1, keepdims=True) + EPS) * qscale
        t = t * jnp.where(is_qk, inv, 1.0)
        o_ref[0, :, hh * DN_DK:(hh + 1) * DN_DK] = t.astype(BF16)


_GATE_ROWS = 4 * CHUNK


def _dn_gate_kernel(x_ref, w_ref, alog_ref, dtb_ref, o_ref, w_s):
    @pl.when((pl.program_id(0) == 0) & (pl.program_id(1) == 0))
    def _():
        w_s[...] = w_ref[0].astype(BF16)

    ba = _dot(x_ref[0], w_s[...])
    beta = jax.nn.sigmoid(ba)
    g = -jnp.exp(alog_ref[...]) * jax.nn.softplus(ba + dtb_ref[...])
    ii = lax.broadcasted_iota(jnp.int32, (_GATE_ROWS, _GATE_ROWS), 0)
    jj = lax.broadcasted_iota(jnp.int32, (_GATE_ROWS, _GATE_ROWS), 1)
    same_chunk = (ii // CHUNK) == (jj // CHUNK)
    tri_f = jnp.where(same_chunk & (ii >= jj), 1.0, 0.0).astype(BF16)
    tri_b = jnp.where(same_chunk & (ii <= jj), 1.0, 0.0).astype(BF16)
    g1 = g.astype(BF16)
    r1 = g - g1.astype(F32)
    g2 = r1.astype(BF16)
    g3 = (r1 - g2.astype(F32)).astype(BF16)
    cs_f = _dot(tri_f, g1) + _dot(tri_f, g2) + _dot(tri_f, g3)
    cs_b = _dot(tri_b, g1) + _dot(tri_b, g2) + _dot(tri_b, g3)
    col = lax.broadcasted_iota(jnp.int32, (_GATE_ROWS, 128), 1)
    o_ref[0] = jnp.where(col < 2 * DN_V_HEADS, beta, jnp.where(col < 3 * DN_V_HEADS, cs_f, cs_b))


def _dn_project(n, w_in, conv_w, a_log, dt_bias, i):
    tn = 256
    qkv = _proj_call(functools.partial(_dn_conv_kernel, tn=tn), n, w_in, i, 0, DN_CONV_CH, tn, (conv_w,),
                     [pl.BlockSpec((1, DN_SHORT_K, tn), lambda b, j: (i, 0, j))], BF16, "dn_qkv",
                     scratch=[pltpu.VMEM((_PAD_ROWS, tn), F32)])
    z = _proj_call(_plain_kernel, n, w_in, i, DN_CONV_CH // tn, DN_V_W, tn, (), [], BF16, "dn_z")
    zeros = jnp.zeros((2 * DN_V_HEADS,), F32)
    alog = jnp.concatenate([zeros, a_log[i].reshape(-1)]).reshape(1, 128)
    dtb = jnp.concatenate([zeros, dt_bias[i].reshape(-1)]).reshape(1, 128)
    gates = pl.pallas_call(
        _dn_gate_kernel,
        grid=(NB, S // _GATE_ROWS),
        in_specs=[
            pl.BlockSpec((1, _GATE_ROWS, D), lambda b, c: (b, c, 0)),
            pl.BlockSpec((1, D, 128), lambda b, c: (i, 0, (DN_CONV_CH + DN_V_W) // 128)),
            pl.BlockSpec((1, 128), lambda b, c: (0, 0)),
            pl.BlockSpec((1, 128), lambda b, c: (0, 0)),
        ],
        out_specs=pl.BlockSpec((1, _GATE_ROWS, 128), lambda b, c: (b, c, 0)),
        out_shape=jax.ShapeDtypeStruct((NB, S, 128), F32),
        scratch_shapes=[pltpu.VMEM((D, 128), BF16)],
        compiler_params=_params(("arbitrary", "arbitrary")),
        name="dn_gates",
    )(n, w_in, alog, dtb)
    return qkv, z, gates


_DN_G = 32
_DN_HG = DN_V_HEADS // _DN_G
_DN_REP = DN_V_HEADS // DN_K_HEADS


def _bmm(a, b):
    return lax.dot_general(a, b, (((2,), (1,)), ((0,), (0,))), preferred_element_type=F32)


def _bmm_nt(a, b):
    return lax.dot_general(a, b, (((2,), (2,)), ((0,), (0,))), preferred_element_type=F32)


def _bmm_tn(a, b):
    return lax.dot_general(a, b, (((1,), (1,)), ((0,), (0,))), preferred_element_type=F32)


def _hi_lo(x):
    hi = x.astype(BF16).astype(F32)
    return hi, x - hi


def _bmm_f32(a, b):
    ah, al = _hi_lo(a)
    bh, bl = _hi_lo(b)
    lhs = jnp.concatenate([ah, al, ah], axis=2).astype(BF16)
    rhs = jnp.concatenate([bh, bh, bl], axis=1).astype(BF16)
    return _bmm(lhs, rhs)


def _bmm_f32_lhs(a, b):
    ah, al = _hi_lo(a)
    lhs = jnp.concatenate([ah, al], axis=2).astype(BF16)
    return _bmm(lhs, jnp.concatenate([b, b], axis=1))


def _dn_scan_kernel(q_ref, k_ref, v_ref, g1_ref, g2_ref, o_ref, s_ref):
    d = pl.program_id(1)
    step = pl.program_id(3)
    ng, nk = _DN_G, _DN_G // _DN_REP

    @pl.when(step == 0)
    def _():
        s_ref[...] = jnp.zeros_like(s_ref)

    fwd = d == 0
    ii = lax.broadcasted_iota(jnp.int32, (CHUNK, CHUNK), 0)
    jj = lax.broadcasted_iota(jnp.int32, (CHUNK, CHUNK), 1)
    later = (ii - jj) * jnp.where(fwd, 1, -1)
    incl = later >= 0
    strict = jnp.where(later > 0, 1.0, 0.0)
    eye = jnp.where(ii == jj, 1.0, 0.0)

    def same_block(size):
        return jnp.where((ii // size) == (jj // size), 1.0, 0.0)

    g1 = g1_ref[0, 0, 0]
    g2 = g2_ref[0, 0, 0, 0]
    q3 = jnp.stack([q_ref[0, :, i * DN_DK:(i + 1) * DN_DK] for i in range(nk)])
    k3 = jnp.stack([k_ref[0, :, i * DN_DK:(i + 1) * DN_DK] for i in range(nk)])
    v3 = jnp.stack([v_ref[0, :, e * DN_DV:(e + 1) * DN_DV] for e in range(ng)])
    beta_r = jnp.stack([g2[e:e + 1, :] for e in range(ng)])
    gc_r = jnp.stack([g2[ng + e:ng + e + 1, :] for e in range(ng)])
    beta_c = jnp.stack([jnp.broadcast_to(g1[:, e:e + 1], (CHUNK, CHUNK)) for e in range(ng)])
    gc_c = jnp.stack([jnp.broadcast_to(g1[:, ng + e:ng + e + 1], (CHUNK, DN_DK)) for e in range(ng)])
    g_last = jnp.where(fwd, gc_r[:, :, CHUNK - 1:CHUNK], gc_r[:, :, 0:1])

    decay = jnp.exp(jnp.where(incl, gc_c[:, :, :CHUNK] - gc_r, -jnp.inf))
    kk = jnp.repeat(_bmm_nt(k3, k3), _DN_REP, axis=0)
    qk = jnp.repeat(_bmm_nt(q3, k3), _DN_REP, axis=0)
    a = kk * (beta_c * decay) * strict

    def mm(x, y):
        return _bmm(x.astype(BF16), y.astype(BF16))

    d4 = a * same_block(4)
    imd = eye - d4
    t = imd + mm(imd, mm(d4, d4))
    size = 4
    while size < CHUNK:
        n_s = a * (same_block(2 * size) - same_block(size))
        t = t - mm(t, mm(n_s, t))
        size *= 2
    t = t + mm(t, eye - t - _bmm_f32(a, t))

    k_v = jnp.repeat(k3, _DN_REP, axis=0)
    tb = t * beta_r
    u = _bmm_f32_lhs(tb, v3)
    w = _bmm_f32_lhs(tb * jnp.exp(gc_r), k_v)

    st = s_ref[...]
    stb = st.astype(BF16)
    qg = (jnp.repeat(q3, _DN_REP, axis=0).astype(F32) * jnp.exp(gc_c)).astype(BF16)
    ws = _bmm(jnp.concatenate([w.astype(BF16), qg], axis=1), stb)
    v_new = u - ws[:, :CHUNK]
    vnb = v_new.astype(BF16)
    o = ws[:, CHUNK:] + _bmm((qk * decay).astype(BF16), vnb)
    for e in range(ng):
        o_ref[0, 0, :, e * DN_DV:(e + 1) * DN_DV] = o[e]
    kd = (k_v.astype(F32) * jnp.exp(g_last - gc_c)).astype(BF16)
    s_ref[...] = st * jnp.exp(g_last) + _bmm_tn(kd, vnb)


def _dn_scan(qkv, gates):
    hpg = DN_V_HEADS // _DN_HG
    kw = (DN_K_HEADS // _DN_HG) * DN_DK
    vw = hpg * DN_DV
    bt = gates[..., :2 * DN_V_HEADS].reshape(NB, S, 2, _DN_HG, hpg)
    gc = gates[..., 2 * DN_V_HEADS:].reshape(NB, S, 2, _DN_HG, hpg)
    g1 = jnp.transpose(jnp.concatenate([bt, gc], axis=-1), (0, 2, 3, 1, 4))
    g2 = jnp.swapaxes(g1.reshape(NB, 2, _DN_HG, NCH, CHUNK, 2 * hpg), -1, -2)

    def chunk(d, s):
        return jnp.where(d == 0, s, jnp.where(s < CTX_CH, CTX_CH - 1 - s, NCH + CTX_CH - 1 - s))

    return pl.pallas_call(
        _dn_scan_kernel,
        grid=(NB, 2, _DN_HG, NCH),
        in_specs=[
            pl.BlockSpec((1, CHUNK, kw), lambda b, d, g, s: (b, chunk(d, s), g)),
            pl.BlockSpec((1, CHUNK, kw), lambda b, d, g, s: (b, chunk(d, s), DN_QK_W // kw + g)),
            pl.BlockSpec((1, CHUNK, vw), lambda b, d, g, s: (b, chunk(d, s), 2 * DN_QK_W // vw + g)),
            pl.BlockSpec((1, 1, 1, CHUNK, 2 * hpg), lambda b, d, g, s: (b, d, g, chunk(d, s), 0)),
            pl.BlockSpec((1, 1, 1, 1, 2 * hpg, CHUNK), lambda b, d, g, s: (b, d, g, chunk(d, s), 0, 0)),
        ],
        out_specs=pl.BlockSpec((1, 1, CHUNK, vw), lambda b, d, g, s: (b, d, chunk(d, s), g)),
        out_shape=jax.ShapeDtypeStruct((NB, 2, S, DN_V_W), F32),
        scratch_shapes=[pltpu.VMEM((hpg, DN_DK, DN_DV), F32)],
        compiler_params=_params(("parallel", "parallel", "parallel", "arbitrary")),
        name="dn_scan",
    )(qkv, qkv, qkv, g1, g2)


def _dn_out_kernel(o_ref, z_ref, g_ref, y_ref):
    g = g_ref[...]
    for hh in range(o_ref.shape[-1] // DN_DV):
        sl = slice(hh * DN_DV, (hh + 1) * DN_DV)
        o = o_ref[0, 0, :, sl] + o_ref[0, 1, :, sl]
        o = o * lax.rsqrt(jnp.mean(o * o, axis=-1, keepdims=True) + EPS) * g
        y_ref[0, :, sl] = (o * _silu(z_ref[0, :, sl].astype(F32))).astype(BF16)


def _dn_gated_norm(o, z, norm_g):
    ts, tw = 256, 1024
    return pl.pallas_call(
        _dn_out_kernel,
        grid=(NB, S // ts, DN_V_W // tw),
        in_specs=[
            pl.BlockSpec((1, 2, ts, tw), lambda b, t, c: (b, 0, t, c)),
            pl.BlockSpec((1, ts, tw), lambda b, t, c: (b, t, c)),
            pl.BlockSpec((1, DN_DV), lambda b, t, c: (0, 0)),
        ],
        out_specs=pl.BlockSpec((1, ts, tw), lambda b, t, c: (b, t, c)),
        out_shape=jax.ShapeDtypeStruct((NB, S, DN_V_W), BF16),
        compiler_params=_params(("parallel", "parallel", "parallel")),
        name="dn_gated_norm",
    )(o, z, norm_g.reshape(1, DN_DV))


def kernel(x, c, ctx, c_ctx, w_mod, b_mod, norm1_g, norm2_g, ffn_w_up, ffn_conv_w, ffn_conv_b, ffn_w_down, even_w_in, even_w_out, attn_q_norm_g, attn_k_norm_g, attn_sink, conv_dw_w, conv_dw_b, conv_ln_g, conv_ln_b, dn_w_in, dn_conv_w, dn_a_log, dn_dt_bias, dn_norm_g, dn_w_out):
    cvec = jnp.concatenate([c, c_ctx[None, :], jnp.zeros((8 - NB - 1, D), F32)], axis=0)
    mods = _modulation(cvec, w_mod, b_mod).reshape(DEPTH, 8, N_MOD, D)
    rope = _rope_tables()
    h = jnp.concatenate([ctx, x], axis=1)
    m = NB * S
    for layer in range(DEPTH):
        i = layer // 2
        sh1, sc1, g1, sh2, sc2, g2 = [mods[layer, :, j, :] for j in range(N_MOD)]
        n = _norm_mod(h, norm1_g[layer], sc1.reshape(8, 1, D), sh1.reshape(8, 1, D))
        hf = h.reshape(m, D)
        if layer % 2 == 0:
            qk, v, glu = _even_proj(n, even_w_in, attn_q_norm_g[i], attn_k_norm_g[i], rope, i)
            att = _attention(qk, v, attn_sink, i)
            cv = _conformer(glu, conv_dw_w, conv_dw_b, conv_ln_g, conv_ln_b, i)
            hf = _mm_res([att.reshape(m, A_Q_W), cv.reshape(m, CONV_CH)], even_w_out, i, hf, g1, "even_out")
        else:
            qkv, z, gates = _dn_project(n, dn_w_in, dn_conv_w, dn_a_log, dn_dt_bias, i)
            o = _dn_scan(qkv, gates)
            y = _dn_gated_norm(o, z, dn_norm_g[i])
            hf = _mm_res([y.reshape(m, DN_V_W)], dn_w_out, i, hf, g1, "dn_out")
        h = hf.reshape(NB, S, D)
        n2 = _norm_mod(h, norm2_g[layer], sc2.reshape(8, 1, D), sh2.reshape(8, 1, D))
        f = _ffn_up(n2, ffn_w_up, ffn_conv_w, ffn_conv_b, layer)
        h = _mm_res([f.reshape(m, D_FF)], ffn_w_down, layer, h.reshape(m, D), g2, "ffn_down").reshape(NB, S, D)
    return h[:, CTX:, :]
```

```python
import functools
import math

import jax
import jax.numpy as jnp
from jax import lax
from jax.experimental import pallas as pl
from jax.experimental.pallas import tpu as pltpu

F32 = jnp.float32
BF16 = jnp.bfloat16

D = 2048
NB = 4
SEQ = 2048
CTX = 256
S = CTX + SEQ
DEPTH = 4
GRID_W = 64
EPS = 1e-6
N_MOD = 6

HEAD_DIM = 128
A_Q_HEADS = 8
A_KV_HEADS = 2
A_GROUP = 4
A_Q_W = 1024
A_KV_W = 256
WINDOW = 128
ABLK = 128
ROPE_THETA = 10000.0
CONV_CH = 1024
CONV_K = 31
EVEN_IN_W = 3584

DN_DK = 128
DN_DV = 128
DN_K_HEADS = 16
DN_V_HEADS = 32
DN_QK_W = 2048
DN_V_W = 4096
DN_CONV_CH = 8192
DN_BA_W = 128
DN_SHORT_K = 5
CHUNK = 64
NCH = S // CHUNK
CTX_CH = CTX // CHUNK

D_FF = 5632
FFN_CONV_K = 3

VMEM_LIMIT = 58 * 1024 * 1024


def _params(sem, vmem=VMEM_LIMIT):
    return pltpu.CompilerParams(dimension_semantics=sem, vmem_limit_bytes=vmem)


def _silu(x):
    return x * jax.nn.sigmoid(x)


def _dot(a, b):
    return jnp.dot(a, b, preferred_element_type=F32)


def _dot_nt(a, b):
    return lax.dot_general(a, b, (((1,), (1,)), ((), ())), preferred_element_type=F32)


def _dot_tn(a, b):
    return lax.dot_general(a, b, (((0,), (0,)), ((), ())), preferred_element_type=F32)


_SUB = 8


def _seg_conv(y, cw):
    taps, n = cw.shape
    left = (taps - 1) // 2
    assert left < _SUB
    y3 = y.reshape(S // _SUB, _SUB, n)
    sub = lax.broadcasted_iota(jnp.int32, (1, _SUB, n), 1)
    nc = CTX // _SUB
    zero = jnp.zeros((1, _SUB, n), F32)
    acc = cw[left:left + 1] * y3
    for k in range(taps):
        s = k - left
        if s == 0:
            continue
        r = pltpu.roll(y3, (-s) % _SUB, 1)
        if s > 0:
            other = jnp.concatenate([r[1:nc], zero, r[nc + 1:], zero], axis=0)
            shifted = jnp.where(sub < _SUB - s, r, other)
        else:
            other = jnp.concatenate([zero, r[:nc - 1], zero, r[nc:-1]], axis=0)
            shifted = jnp.where(sub >= -s, r, other)
        acc = acc + cw[k:k + 1] * shifted
    return acc.reshape(S, n)


def _mod_kernel(c_ref, w_ref, b_ref, o_ref):
    a = _silu(c_ref[...]).astype(BF16)
    o_ref[0] = _dot(a, w_ref[0].astype(BF16)) + b_ref[0]


def _modulation(cvec, w_mod, b_mod):
    tn = 1024
    n = N_MOD * D
    return pl.pallas_call(
        _mod_kernel,
        grid=(DEPTH, n // tn),
        in_specs=[
            pl.BlockSpec((8, D), lambda l, j: (0, 0)),
            pl.BlockSpec((1, D, tn), lambda l, j: (l, 0, j)),
            pl.BlockSpec((1, 1, tn), lambda l, j: (l, 0, j)),
        ],
        out_specs=pl.BlockSpec((1, 8, tn), lambda l, j: (l, 0, j)),
        out_shape=jax.ShapeDtypeStruct((DEPTH, 8, n), F32),
        compiler_params=_params(("parallel", "parallel")),
        name="adaln_mod",
    )(cvec, w_mod, b_mod.reshape(DEPTH, 1, n))


def _norm_mod_kernel(h_ref, g_ref, sc_ref, sh_ref, o_ref):
    x = h_ref[0]
    y = x * lax.rsqrt(jnp.mean(x * x, axis=-1, keepdims=True) + EPS) * g_ref[...]
    o_ref[0] = (y * (1.0 + sc_ref[0]) + sh_ref[0]).astype(BF16)


def _norm_mod(h, g, sc, sh):
    ts = CTX
    mod_spec = pl.BlockSpec((1, 1, D), lambda b, t: (jnp.where(t == 0, NB, b), 0, 0))
    return pl.pallas_call(
        _norm_mod_kernel,
        grid=(NB, S // ts),
        in_specs=[
            pl.BlockSpec((1, ts, D), lambda b, t: (b, t, 0)),
            pl.BlockSpec((1, D), lambda b, t: (0, 0)),
            mod_spec,
            mod_spec,
        ],
        out_specs=pl.BlockSpec((1, ts, D), lambda b, t: (b, t, 0)),
        out_shape=jax.ShapeDtypeStruct((NB, S, D), BF16),
        compiler_params=_params(("parallel", "parallel")),
        name="norm_mod",
    )(h, g.reshape(1, D), sc, sh)


def _xw(x_ref, w_ref):
    return _dot(x_ref[0], w_ref[0].astype(BF16))


def _proj_call(kernel, x, w, layer, col_blk0, n_out, tn, extra, extra_specs, out_dtype, name,
               w_blk_offsets=(0,), scratch=()):
    k = x.shape[-1]
    w_specs = [pl.BlockSpec((1, k, tn), functools.partial(lambda b, j, off: (layer, 0, col_blk0 + off + j), off=off))
               for off in w_blk_offsets]
    return pl.pallas_call(
        kernel,
        grid=(NB, n_out // tn),
        in_specs=[pl.BlockSpec((1, S, k), lambda b, j: (b, 0, 0))] + w_specs + extra_specs,
        out_specs=pl.BlockSpec((1, S, tn), lambda b, j: (b, 0, j)),
        out_shape=jax.ShapeDtypeStruct((NB, S, n_out), out_dtype),
        scratch_shapes=list(scratch),
        compiler_params=_params(("parallel", "parallel")),
        name=name,
    )(x, *([w] * len(w_blk_offsets)), *extra)


def _ffn_up_kernel(x_ref, wg_ref, wv_ref, cw_ref, cb_ref, o_ref):
    gate = _xw(x_ref, wg_ref)
    val = _xw(x_ref, wv_ref)
    conv = _seg_conv(gate, cw_ref[0]) + cb_ref[0]
    o_ref[0] = (_silu(conv) * val).astype(BF16)


def _ffn_up(n, w_up, conv_w, conv_b, layer):
    tn = 256
    nblk = D_FF // tn
    extra_specs = [
        pl.BlockSpec((1, FFN_CONV_K, tn), lambda b, j: (layer, 0, j)),
        pl.BlockSpec((1, 1, tn), lambda b, j: (layer, 0, j)),
    ]
    return _proj_call(_ffn_up_kernel, n, w_up, layer, 0, D_FF, tn,
                      (conv_w, conv_b.reshape(DEPTH, 1, D_FF)), extra_specs, BF16, "ffn_up",
                      w_blk_offsets=(0, nblk))


def _qk_kernel(x_ref, w_ref, g_ref, cos_ref, sa_ref, sb_ref, o_ref):
    y = _xw(x_ref, w_ref)
    g = g_ref[0]
    cos, sa, sb = cos_ref[...], sa_ref[...], sb_ref[...]
    for hh in range(y.shape[1] // HEAD_DIM):
        t = y[:, hh * HEAD_DIM:(hh + 1) * HEAD_DIM]
        t = t * lax.rsqrt(jnp.mean(t * t, axis=-1, keepdims=True) + EPS) * g
        r = t * cos + pltpu.roll(t, 96, 1) * sa + pltpu.roll(t, 32, 1) * sb
        o_ref[0, :, hh * HEAD_DIM:(hh + 1) * HEAD_DIM] = r.astype(BF16)


def _plain_kernel(x_ref, w_ref, o_ref):
    o_ref[0] = _xw(x_ref, w_ref).astype(o_ref.dtype)


def _glu_kernel(x_ref, wa_ref, wb_ref, o_ref):
    a = _xw(x_ref, wa_ref)
    b = _xw(x_ref, wb_ref)
    o_ref[0] = (a * jax.nn.sigmoid(b)).astype(BF16)


def _rope_tables():
    row = jnp.repeat(jnp.arange(SEQ // GRID_W, dtype=F32), GRID_W)
    col = jnp.tile(jnp.arange(GRID_W, dtype=F32), SEQ // GRID_W)
    half = HEAD_DIM // 2
    inv_freq = ROPE_THETA ** (-jnp.arange(0, half, 2, dtype=F32) / half)
    ang_r = row[:, None] * inv_freq
    ang_c = col[:, None] * inv_freq
    ang = jnp.concatenate([ang_r, ang_r, ang_c, ang_c], axis=-1)
    cos, sin = jnp.cos(ang), jnp.sin(ang)
    quarter = (jnp.arange(HEAD_DIM) // (HEAD_DIM // 4)) % 2
    sa = jnp.where(quarter == 0, -sin, 0.0)
    sb = jnp.where(quarter == 1, sin, 0.0)
    pad1 = jnp.ones((CTX, HEAD_DIM), F32)
    pad0 = jnp.zeros((CTX, HEAD_DIM), F32)
    return (jnp.concatenate([pad1, cos]), jnp.concatenate([pad0, sa]), jnp.concatenate([pad0, sb]))


def _even_proj(n, w_in, q_g, k_g, rope, i):
    tn = 256
    gains = jnp.stack([q_g, k_g]).reshape(2, 1, HEAD_DIM)
    tab_spec = pl.BlockSpec((S, HEAD_DIM), lambda b, j: (0, 0))
    qk = _proj_call(_qk_kernel, n, w_in, i, 0, A_Q_W + A_KV_W, tn, (gains,) + rope,
                    [pl.BlockSpec((1, 1, HEAD_DIM), lambda b, j: (jnp.where(j < A_Q_W // tn, 0, 1), 0, 0)),
                     tab_spec, tab_spec, tab_spec], BF16, "even_qk")
    v = _proj_call(_plain_kernel, n, w_in, i, (A_Q_W + A_KV_W) // tn, A_KV_W, tn, (), [], BF16, "even_v")
    glu0 = (A_Q_W + 2 * A_KV_W) // tn
    glu = _proj_call(_glu_kernel, n, w_in, i, glu0, CONV_CH, tn, (), [], BF16, "even_glu",
                     w_blk_offsets=(0, CONV_CH // tn))
    return qk, v, glu


def _mm_res_kernel(*refs, n_x, k_sizes, tm):
    x_refs = refs[:n_x]
    w_ref, h_ref, g_ref, o_ref = refs[n_x:]
    i = pl.program_id(0)
    acc = None
    off = 0
    for x_ref, kx in zip(x_refs, k_sizes):
        part = _dot(x_ref[...], w_ref[0, off:off + kx, :].astype(BF16))
        acc = part if acc is None else acc + part
        off += kx
    tiles_per_sample = S // tm
    b = i // tiles_per_sample
    row = lax.broadcasted_iota(jnp.int32, (tm, 1), 0) + (i % tiles_per_sample) * tm
    gate = jnp.where(row < CTX, g_ref[NB:NB + 1, :], g_ref[pl.ds(b, 1), :])
    o_ref[...] = h_ref[...] + gate * acc


def _mm_res(xs, w, layer, h, gate, name):
    tn, tm = 256, S // 2
    k_sizes = tuple(x.shape[-1] for x in xs)
    k = sum(k_sizes)
    m = NB * S
    kern = functools.partial(_mm_res_kernel, n_x=len(xs), k_sizes=k_sizes, tm=tm)
    return pl.pallas_call(
        kern,
        grid=(m // tm, D // tn),
        in_specs=[pl.BlockSpec((tm, kx), lambda i, j: (i, 0)) for kx in k_sizes] + [
            pl.BlockSpec((1, k, tn), lambda i, j: (layer, 0, j)),
            pl.BlockSpec((tm, tn), lambda i, j: (i, j)),
            pl.BlockSpec((8, tn), lambda i, j: (0, j)),
        ],
        out_specs=pl.BlockSpec((tm, tn), lambda i, j: (i, j)),
        out_shape=jax.ShapeDtypeStruct((m, D), F32),
        compiler_params=_params(("parallel", "parallel")),
        name=name,
    )(*xs, w, h, gate)


_CPAD = 16
_CROWS = _CPAD + CTX + _CPAD + SEQ + _CPAD
_CBLK = 256


def _conformer_kernel(x_ref, w_ref, b_ref, g_ref, beta_ref, o_ref, pad_s):
    zeros = jnp.zeros((_CPAD, 128), F32)
    pad_s[0:_CPAD, :] = zeros
    pad_s[_CPAD + CTX:2 * _CPAD + CTX, :] = zeros
    pad_s[_CROWS - _CPAD:_CROWS, :] = zeros
    pad_s[_CPAD:_CPAD + CTX, :] = x_ref[0, 0:CTX, :].astype(F32)
    pad_s[2 * _CPAD + CTX:2 * _CPAD + S, :] = x_ref[0, CTX:S, :].astype(F32)
    w = w_ref[0]
    left = (CONV_K - 1) // 2
    for blk in range(S // _CBLK):
        r0 = blk * _CBLK
        p0 = r0 + (_CPAD if r0 < CTX else 2 * _CPAD)
        acc = jnp.zeros((_CBLK, 128), F32)
        for k in range(CONV_K):
            acc = acc + w[k:k + 1, :] * pad_s[p0 + k - left:p0 + k - left + _CBLK, :]
        hcv = acc + b_ref[0]
        mu = jnp.mean(hcv, axis=-1, keepdims=True)
        dlt = hcv - mu
        var = jnp.mean(dlt * dlt, axis=-1, keepdims=True)
        hn = dlt * lax.rsqrt(var + EPS) * g_ref[0] + beta_ref[0]
        o_ref[0, r0:r0 + _CBLK, :] = _silu(hn).astype(BF16)


def _conformer(glu, dw_w, dw_b, ln_g, ln_b, i):
    groups = CONV_CH // 128
    vec = lambda a: a.reshape(a.shape[0], 1, CONV_CH)
    vspec = pl.BlockSpec((1, 1, 128), lambda b, c: (i, 0, c))
    return pl.pallas_call(
        _conformer_kernel,
        grid=(NB, groups),
        in_specs=[
            pl.BlockSpec((1, S, 128), lambda b, c: (b, 0, c)),
            pl.BlockSpec((1, CONV_K, 128), lambda b, c: (i, 0, c)),
            vspec, vspec, vspec,
        ],
        out_specs=pl.BlockSpec((1, S, 128), lambda b, c: (b, 0, c)),
        out_shape=jax.ShapeDtypeStruct((NB, S, CONV_CH), BF16),
        scratch_shapes=[pltpu.VMEM((_CROWS, 128), F32)],
        compiler_params=_params(("parallel", "parallel")),
        name="conformer_conv",
    )(glu, dw_w, vec(dw_b), vec(ln_g), vec(ln_b))


def _attn_kernel(sink_ref, q_ref, kc_ref, vc_ref, *rest, band, layer_row):
    if band:
        kp_ref, kn_ref, kx_ref, vp_ref, vn_ref, vx_ref, o_ref = rest
    else:
        (o_ref,) = rest
    h = pl.program_id(1)
    nq = q_ref.shape[1]
    q = jnp.concatenate([q_ref[0, :, g * HEAD_DIM:(g + 1) * HEAD_DIM] for g in range(A_GROUP)], axis=0)
    scale = HEAD_DIM ** -0.5
    if band:
        n = pl.program_id(2)
        kcat = jnp.concatenate([kc_ref[0], kp_ref[0], kn_ref[0], kx_ref[0]], axis=0)
        vcat = jnp.concatenate([vc_ref[0], vp_ref[0], vn_ref[0], vx_ref[0]], axis=0)
    else:
        kcat, vcat = kc_ref[0], vc_ref[0]
    s = _dot_nt(q, kcat) * scale
    rows = A_GROUP * nq
    if band:
        nk = CTX + 3 * ABLK
        r = lax.broadcasted_iota(jnp.int32, (rows, nk), 0) % nq
        c = lax.broadcasted_iota(jnp.int32, (rows, nk), 1) - CTX
        rel = c - ABLK - r
        kpos = (n - 1) * ABLK + c
        ok = (c < 0) | ((jnp.abs(rel) <= WINDOW) & (kpos >= 0) & (kpos < SEQ))
        s = jnp.where(ok, s, -jnp.inf)
    gidx = lax.broadcasted_iota(jnp.int32, (rows, 1), 0) // nq
    sink = jnp.zeros((rows, 1), F32)
    for g in range(A_GROUP):
        sink = jnp.where(gidx == g, sink_ref[layer_row, h * A_GROUP + g], sink)
    m = jnp.maximum(jnp.max(s, axis=-1, keepdims=True), sink)
    p = jnp.exp(s - m)
    den = jnp.sum(p, axis=-1, keepdims=True) + jnp.exp(sink - m)
    o = _dot(p.astype(BF16), vcat) / den
    for g in range(A_GROUP):
        o_ref[0, :, g * HEAD_DIM:(g + 1) * HEAD_DIM] = o[g * nq:(g + 1) * nq].astype(BF16)


def _attention(qk, v, sink, i):
    gw = A_GROUP * HEAD_DIM
    kcol0 = A_Q_W // HEAD_DIM
    cb = CTX // ABLK
    nb = SEQ // ABLK
    smem = pl.BlockSpec(memory_space=pltpu.SMEM)

    def blk(rows, col_fn, row_fn):
        return pl.BlockSpec((1, rows, HEAD_DIM), lambda b, h, n: (b, row_fn(n), col_fn(h)))

    kcol = lambda h: kcol0 + h
    vcol = lambda h: h
    prev_r = lambda n: cb + jnp.maximum(n - 1, 0)
    cur_r = lambda n: cb + n
    next_r = lambda n: cb + jnp.minimum(n + 1, nb - 1)
    zero_r = lambda n: 0
    lat = pl.pallas_call(
        functools.partial(_attn_kernel, band=True, layer_row=i),
        grid=(NB, A_KV_HEADS, nb),
        in_specs=[
            smem,
            pl.BlockSpec((1, ABLK, gw), lambda b, h, n: (b, cb + n, h)),
            blk(CTX, kcol, zero_r), blk(CTX, vcol, zero_r),
            blk(ABLK, kcol, prev_r), blk(ABLK, kcol, cur_r), blk(ABLK, kcol, next_r),
            blk(ABLK, vcol, prev_r), blk(ABLK, vcol, cur_r), blk(ABLK, vcol, next_r),
        ],
        out_specs=pl.BlockSpec((1, ABLK, gw), lambda b, h, n: (b, n, h)),
        out_shape=jax.ShapeDtypeStruct((NB, SEQ, A_Q_W), BF16),
        compiler_params=_params(("parallel", "parallel", "parallel")),
        name="attn_latent",
    )(sink, qk, qk, v, qk, qk, qk, v, v, v)
    ctx = pl.pallas_call(
        functools.partial(_attn_kernel, band=False, layer_row=i),
        grid=(NB, A_KV_HEADS),
        in_specs=[
            smem,
            pl.BlockSpec((1, CTX, gw), lambda b, h: (b, 0, h)),
            pl.BlockSpec((1, CTX, HEAD_DIM), lambda b, h: (b, 0, kcol0 + h)),
            pl.BlockSpec((1, CTX, HEAD_DIM), lambda b, h: (b, 0, h)),
        ],
        out_specs=pl.BlockSpec((1, CTX, gw), lambda b, h: (b, 0, h)),
        out_shape=jax.ShapeDtypeStruct((NB, CTX, A_Q_W), BF16),
        compiler_params=_params(("parallel", "parallel")),
        name="attn_context",
    )(sink, qk, qk, v)
    return jnp.concatenate([ctx, lat], axis=1)


def _dn_conv_kernel(x_ref, w_ref, cw_ref, o_ref, *, tn, sub, l2norm):
    j = pl.program_id(1)
    x = x_ref[0]
    for s in range(sub):
        y = _dot(x, w_ref[0, :, s * tn:(s + 1) * tn].astype(BF16))
        a = _silu(_seg_conv(y, cw_ref[0, :, s * tn:(s + 1) * tn]))
        if not l2norm:
            o_ref[0, :, s * tn:(s + 1) * tn] = a.astype(BF16)
            continue
        qscale = jnp.where((j * sub + s) * tn < DN_QK_W, DN_DK ** -0.5, 1.0)
        for hh in range(tn // DN_DK):
            c0 = s * tn + hh * DN_DK
            t = a[:, hh * DN_DK:(hh + 1) * DN_DK]
            inv = lax.rsqrt(jnp.sum(t * t, axis=-1, keepdims=True) + EPS) * qscale
            o_ref[0, :, c0:c0 + DN_DK] = (t * inv).astype(BF16)


_GATE_ROWS = 4 * CHUNK


def _dn_gate_kernel(x_ref, w_ref, alog_ref, dtb_ref, o_ref, w_s):
    @pl.when((pl.program_id(0) == 0) & (pl.program_id(1) == 0))
    def _():
        w_s[...] = w_ref[0].astype(BF16)

    ba = _dot(x_ref[0], w_s[...])
    beta = jax.nn.sigmoid(ba)
    g = -jnp.exp(alog_ref[...]) * jax.nn.softplus(ba + dtb_ref[...])
    ii = lax.broadcasted_iota(jnp.int32, (_GATE_ROWS, _GATE_ROWS), 0)
    jj = lax.broadcasted_iota(jnp.int32, (_GATE_ROWS, _GATE_ROWS), 1)
    same_chunk = (ii // CHUNK) == (jj // CHUNK)
    tri_f = jnp.where(same_chunk & (ii >= jj), 1.0, 0.0).astype(BF16)
    tri_b = jnp.where(same_chunk & (ii <= jj), 1.0, 0.0).astype(BF16)
    g1 = g.astype(BF16)
    r1 = g - g1.astype(F32)
    g2 = r1.astype(BF16)
    g3 = (r1 - g2.astype(F32)).astype(BF16)
    cs_f = _dot(tri_f, g1) + _dot(tri_f, g2) + _dot(tri_f, g3)
    cs_b = _dot(tri_b, g1) + _dot(tri_b, g2) + _dot(tri_b, g3)
    col = lax.broadcasted_iota(jnp.int32, (_GATE_ROWS, 128), 1)
    o_ref[0] = jnp.where(col < 2 * DN_V_HEADS, beta, jnp.where(col < 3 * DN_V_HEADS, cs_f, cs_b))


def _dn_project(n, w_in, conv_w, a_log, dt_bias, i):
    tn, sub = 256, 2
    tb = tn * sub

    def conv_proj(col0, width, l2norm, name):
        blk0 = col0 // tb
        return _proj_call(functools.partial(_dn_conv_kernel, tn=tn, sub=sub, l2norm=l2norm), n, w_in, i, blk0,
                          width, tb, (conv_w,),
                          [pl.BlockSpec((1, DN_SHORT_K, tb), lambda b, j: (i, 0, blk0 + j))], BF16, name)

    qk = conv_proj(0, 2 * DN_QK_W, True, "dn_qk")
    vv = conv_proj(2 * DN_QK_W, DN_V_W, False, "dn_v")
    z = _proj_call(_plain_kernel, n, w_in, i, DN_CONV_CH // tn, DN_V_W, tn, (), [], BF16, "dn_z")
    zeros = jnp.zeros((2 * DN_V_HEADS,), F32)
    alog = jnp.concatenate([zeros, a_log[i].reshape(-1)]).reshape(1, 128)
    dtb = jnp.concatenate([zeros, dt_bias[i].reshape(-1)]).reshape(1, 128)
    gates = pl.pallas_call(
        _dn_gate_kernel,
        grid=(NB, S // _GATE_ROWS),
        in_specs=[
            pl.BlockSpec((1, _GATE_ROWS, D), lambda b, c: (b, c, 0)),
            pl.BlockSpec((1, D, 128), lambda b, c: (i, 0, (DN_CONV_CH + DN_V_W) // 128)),
            pl.BlockSpec((1, 128), lambda b, c: (0, 0)),
            pl.BlockSpec((1, 128), lambda b, c: (0, 0)),
        ],
        out_specs=pl.BlockSpec((1, _GATE_ROWS, 128), lambda b, c: (b, c, 0)),
        out_shape=jax.ShapeDtypeStruct((NB, S, 128), F32),
        scratch_shapes=[pltpu.VMEM((D, 128), BF16)],
        compiler_params=_params(("arbitrary", "arbitrary")),
        name="dn_gates",
    )(n, w_in, alog, dtb)
    return qk, vv, z, gates


_DN_G = 32
_DN_HG = DN_V_HEADS // _DN_G
_DN_REP = DN_V_HEADS // DN_K_HEADS


def _bmm(a, b):
    return lax.dot_general(a, b, (((2,), (1,)), ((0,), (0,))), preferred_element_type=F32)


def _bmm_nt(a, b):
    return lax.dot_general(a, b, (((2,), (2,)), ((0,), (0,))), preferred_element_type=F32)


def _bmm_tn(a, b):
    return lax.dot_general(a, b, (((1,), (1,)), ((0,), (0,))), preferred_element_type=F32)


def _hi_lo(x):
    hi = x.astype(BF16).astype(F32)
    return hi, x - hi


def _bmm_f32_lhs(a, b):
    ah, al = _hi_lo(a)
    lhs = jnp.concatenate([ah, al], axis=2).astype(BF16)
    return _bmm(lhs, jnp.concatenate([b, b], axis=1))


def _bmm_f32(a, b):
    ah, al = _hi_lo(a)
    bh, bl = _hi_lo(b)
    lhs = jnp.concatenate([ah, al, ah], axis=2).astype(BF16)
    rhs = jnp.concatenate([bh, bh, bl], axis=1).astype(BF16)
    return _bmm(lhs, rhs)


def _dn_scan_kernel(q_ref, k_ref, v_ref, g1_ref, g2_ref, o_ref, s_ref):
    d = pl.program_id(1)
    step = pl.program_id(3)
    ng, nk = _DN_G, _DN_G // _DN_REP

    @pl.when(step == 0)
    def _():
        s_ref[...] = jnp.zeros_like(s_ref)

    fwd = d == 0
    ii = lax.broadcasted_iota(jnp.int32, (CHUNK, CHUNK), 0)
    jj = lax.broadcasted_iota(jnp.int32, (CHUNK, CHUNK), 1)
    later = (ii - jj) * jnp.where(fwd, 1, -1)
    incl = later >= 0
    strict = jnp.where(later > 0, 1.0, 0.0)
    eye = jnp.where(ii == jj, 1.0, 0.0)

    def same_block(size):
        return jnp.where((ii // size) == (jj // size), 1.0, 0.0)

    g1 = g1_ref[0, 0, 0]
    g2 = g2_ref[0, 0, 0, 0]
    q3 = jnp.stack([q_ref[0, :, i * DN_DK:(i + 1) * DN_DK] for i in range(nk)])
    k3 = jnp.stack([k_ref[0, :, i * DN_DK:(i + 1) * DN_DK] for i in range(nk)])
    v3 = jnp.stack([v_ref[0, :, e * DN_DV:(e + 1) * DN_DV] for e in range(ng)])
    beta_r = jnp.stack([g2[e:e + 1, :] for e in range(ng)])
    gc_r = jnp.stack([g2[ng + e:ng + e + 1, :] for e in range(ng)])
    beta_c = jnp.stack([jnp.broadcast_to(g1[:, e:e + 1], (CHUNK, CHUNK)) for e in range(ng)])
    gc_c = jnp.stack([jnp.broadcast_to(g1[:, ng + e:ng + e + 1], (CHUNK, DN_DK)) for e in range(ng)])
    g_last = jnp.where(fwd, gc_r[:, :, CHUNK - 1:CHUNK], gc_r[:, :, 0:1])

    decay = jnp.exp(jnp.where(incl, gc_c[:, :, :CHUNK] - gc_r, -jnp.inf))
    kk = jnp.repeat(_bmm_nt(k3, k3), _DN_REP, axis=0)
    qk = jnp.repeat(_bmm_nt(q3, k3), _DN_REP, axis=0)
    a = kk * (beta_c * decay) * strict

    def mm(x, y):
        return _bmm(x.astype(BF16), y.astype(BF16))

    d4 = a * same_block(4)
    imd = eye - d4
    t = imd + mm(imd, mm(d4, d4))
    size = 4
    while size < CHUNK:
        n_s = a * (same_block(2 * size) - same_block(size))
        t = t - mm(t, mm(n_s, t))
        size *= 2
    t = t + mm(t, eye - t - _bmm_f32(a, t))

    k_v = jnp.repeat(k3, _DN_REP, axis=0)
    tb = t * beta_r
    u = _bmm_f32_lhs(tb, v3)
    w = _bmm_f32_lhs(tb * jnp.exp(gc_r), k_v)

    st = s_ref[...]
    stb = st.astype(BF16)
    qg = jnp.repeat(q3, _DN_REP, axis=0) * jnp.exp(gc_c).astype(BF16)
    ws = _bmm(jnp.concatenate([w.astype(BF16), qg], axis=1), stb)
    v_new = u - ws[:, :CHUNK]
    vnb = v_new.astype(BF16)
    o = ws[:, CHUNK:] + _bmm((qk * decay).astype(BF16), vnb)
    for e in range(ng):
        o_ref[0, 0, :, e * DN_DV:(e + 1) * DN_DV] = o[e].astype(o_ref.dtype)
    kd = k_v * jnp.exp(g_last - gc_c).astype(BF16)
    s_ref[...] = st * jnp.exp(g_last) + _bmm_tn(kd, vnb)


def _dn_scan(qk, vv, gates):
    hpg = DN_V_HEADS // _DN_HG
    kw = (DN_K_HEADS // _DN_HG) * DN_DK
    vw = hpg * DN_DV
    bt = gates[..., :2 * DN_V_HEADS].reshape(NB, S, 2, _DN_HG, hpg)
    gc = gates[..., 2 * DN_V_HEADS:].reshape(NB, S, 2, _DN_HG, hpg)
    g1 = jnp.transpose(jnp.concatenate([bt, gc], axis=-1), (0, 2, 3, 1, 4))
    g2 = jnp.swapaxes(g1.reshape(NB, 2, _DN_HG, NCH, CHUNK, 2 * hpg), -1, -2)

    def chunk(d, s):
        return jnp.where(d == 0, s, jnp.where(s < CTX_CH, CTX_CH - 1 - s, NCH + CTX_CH - 1 - s))

    return pl.pallas_call(
        _dn_scan_kernel,
        grid=(NB, 2, _DN_HG, NCH),
        in_specs=[
            pl.BlockSpec((1, CHUNK, kw), lambda b, d, g, s: (b, chunk(d, s), g)),
            pl.BlockSpec((1, CHUNK, kw), lambda b, d, g, s: (b, chunk(d, s), DN_QK_W // kw + g)),
            pl.BlockSpec((1, CHUNK, vw), lambda b, d, g, s: (b, chunk(d, s), g)),
            pl.BlockSpec((1, 1, 1, CHUNK, 2 * hpg), lambda b, d, g, s: (b, d, g, chunk(d, s), 0)),
            pl.BlockSpec((1, 1, 1, 1, 2 * hpg, CHUNK), lambda b, d, g, s: (b, d, g, chunk(d, s), 0, 0)),
        ],
        out_specs=pl.BlockSpec((1, 1, CHUNK, vw), lambda b, d, g, s: (b, d, chunk(d, s), g)),
        out_shape=jax.ShapeDtypeStruct((NB, 2, S, DN_V_W), BF16),
        scratch_shapes=[pltpu.VMEM((hpg, DN_DK, DN_DV), F32)],
        compiler_params=_params(("parallel", "parallel", "parallel", "arbitrary")),
        name="dn_scan",
    )(qk, qk, vv, g1, g2)


def _dn_out_kernel(o_ref, z_ref, g_ref, y_ref):
    g = g_ref[...]
    for hh in range(o_ref.shape[-1] // DN_DV):
        sl = slice(hh * DN_DV, (hh + 1) * DN_DV)
        o = o_ref[0, 0, :, sl].astype(F32) + o_ref[0, 1, :, sl].astype(F32)
        o = o * lax.rsqrt(jnp.mean(o * o, axis=-1, keepdims=True) + EPS) * g
        y_ref[0, :, sl] = (o * _silu(z_ref[0, :, sl].astype(F32))).astype(BF16)


def _dn_gated_norm(o, z, norm_g):
    ts, tw = S // 3, 1024
    return pl.pallas_call(
        _dn_out_kernel,
        grid=(NB, S // ts, DN_V_W // tw),
        in_specs=[
            pl.BlockSpec((1, 2, ts, tw), lambda b, t, c: (b, 0, t, c)),
            pl.BlockSpec((1, ts, tw), lambda b, t, c: (b, t, c)),
            pl.BlockSpec((1, DN_DV), lambda b, t, c: (0, 0)),
        ],
        out_specs=pl.BlockSpec((1, ts, tw), lambda b, t, c: (b, t, c)),
        out_shape=jax.ShapeDtypeStruct((NB, S, DN_V_W), BF16),
        compiler_params=_params(("parallel", "parallel", "parallel")),
        name="dn_gated_norm",
    )(o, z, norm_g.reshape(1, DN_DV))


def kernel(x, c, ctx, c_ctx, w_mod, b_mod, norm1_g, norm2_g, ffn_w_up, ffn_conv_w, ffn_conv_b, ffn_w_down, even_w_in, even_w_out, attn_q_norm_g, attn_k_norm_g, attn_sink, conv_dw_w, conv_dw_b, conv_ln_g, conv_ln_b, dn_w_in, dn_conv_w, dn_a_log, dn_dt_bias, dn_norm_g, dn_w_out):
    cvec = jnp.concatenate([c, c_ctx[None, :], jnp.zeros((8 - NB - 1, D), F32)], axis=0)
    mods = _modulation(cvec, w_mod, b_mod).reshape(DEPTH, 8, N_MOD, D)
    rope = _rope_tables()
    h = jnp.concatenate([ctx, x], axis=1)
    m = NB * S
    for layer in range(DEPTH):
        i = layer // 2
        sh1, sc1, g1, sh2, sc2, g2 = [mods[layer, :, j, :] for j in range(N_MOD)]
        n = _norm_mod(h, norm1_g[layer], sc1.reshape(8, 1, D), sh1.reshape(8, 1, D))
        hf = h.reshape(m, D)
        if layer % 2 == 0:
            qk, v, glu = _even_proj(n, even_w_in, attn_q_norm_g[i], attn_k_norm_g[i], rope, i)
            att = _attention(qk, v, attn_sink, i)
            cv = _conformer(glu, conv_dw_w, conv_dw_b, conv_ln_g, conv_ln_b, i)
            hf = _mm_res([att.reshape(m, A_Q_W), cv.reshape(m, CONV_CH)], even_w_out, i, hf, g1, "even_out")
        else:
            qk, vv, z, gates = _dn_project(n, dn_w_in, dn_conv_w, dn_a_log, dn_dt_bias, i)
            o = _dn_scan(qk, vv, gates)
            y = _dn_gated_norm(o, z, dn_norm_g[i])
            hf = _mm_res([y.reshape(m, DN_V_W)], dn_w_out, i, hf, g1, "dn_out")
        h = hf.reshape(NB, S, D)
        n2 = _norm_mod(h, norm2_g[layer], sc2.reshape(8, 1, D), sh2.reshape(8, 1, D))
        f = _ffn_up(n2, ffn_w_up, ffn_conv_w, ffn_conv_b, layer)
        h = _mm_res([f.reshape(m, D_FF)], ffn_w_down, layer, h.reshape(m, D), g2, "ffn_down").reshape(NB, S, D)
    return h[:, CTX:, :]
```

```python
import functools
import math

import jax
import jax.numpy as jnp
from jax import lax
from jax.experimental import pallas as pl
from jax.experimental.pallas import tpu as pltpu

F32 = jnp.float32
BF16 = jnp.bfloat16

D = 2048
NB = 4
SEQ = 2048
CTX = 256
S = CTX + SEQ
DEPTH = 4
GRID_W = 64
EPS = 1e-6
N_MOD = 6

HEAD_DIM = 128
A_Q_HEADS = 8
A_KV_HEADS = 2
A_GROUP = 4
A_Q_W = 1024
A_KV_W = 256
WINDOW = 128
ABLK = 128
ROPE_THETA = 10000.0
CONV_CH = 1024
CONV_K = 31
EVEN_IN_W = 3584

DN_DK = 128
DN_DV = 128
DN_K_HEADS = 16
DN_V_HEADS = 32
DN_QK_W = 2048
DN_V_W = 4096
DN_CONV_CH = 8192
DN_BA_W = 128
DN_SHORT_K = 5
CHUNK = 64
NCH = S // CHUNK
CTX_CH = CTX // CHUNK

D_FF = 5632
FFN_CONV_K = 3

VMEM_LIMIT = 58 * 1024 * 1024


def _params(sem, vmem=VMEM_LIMIT):
    return pltpu.CompilerParams(dimension_semantics=sem, vmem_limit_bytes=vmem)


def _silu(x):
    return x * jax.nn.sigmoid(x)


def _dot(a, b):
    return jnp.dot(a, b, preferred_element_type=F32)


def _dot_nt(a, b):
    return lax.dot_general(a, b, (((1,), (1,)), ((), ())), preferred_element_type=F32)


def _dot_tn(a, b):
    return lax.dot_general(a, b, (((0,), (0,)), ((), ())), preferred_element_type=F32)


_SUB = 8


def _seg_conv(y, cw):
    taps, n = cw.shape
    left = (taps - 1) // 2
    assert left < _SUB
    y3 = y.reshape(S // _SUB, _SUB, n)
    sub = lax.broadcasted_iota(jnp.int32, (1, _SUB, n), 1)
    nc = CTX // _SUB
    zero = jnp.zeros((1, _SUB, n), F32)
    acc = cw[left:left + 1] * y3
    for k in range(taps):
        s = k - left
        if s == 0:
            continue
        r = pltpu.roll(y3, (-s) % _SUB, 1)
        if s > 0:
            other = jnp.concatenate([r[1:nc], zero, r[nc + 1:], zero], axis=0)
            shifted = jnp.where(sub < _SUB - s, r, other)
        else:
            other = jnp.concatenate([zero, r[:nc - 1], zero, r[nc:-1]], axis=0)
            shifted = jnp.where(sub >= -s, r, other)
        acc = acc + cw[k:k + 1] * shifted
    return acc.reshape(S, n)


def _mod_kernel(c_ref, w_ref, b_ref, o_ref):
    a = _silu(c_ref[...]).astype(BF16)
    o_ref[0] = _dot(a, w_ref[0].astype(BF16)) + b_ref[0]


def _modulation(cvec, w_mod, b_mod):
    tn = 1024
    n = N_MOD * D
    return pl.pallas_call(
        _mod_kernel,
        grid=(DEPTH, n // tn),
        in_specs=[
            pl.BlockSpec((8, D), lambda l, j: (0, 0)),
            pl.BlockSpec((1, D, tn), lambda l, j: (l, 0, j)),
            pl.BlockSpec((1, 1, tn), lambda l, j: (l, 0, j)),
        ],
        out_specs=pl.BlockSpec((1, 8, tn), lambda l, j: (l, 0, j)),
        out_shape=jax.ShapeDtypeStruct((DEPTH, 8, n), F32),
        compiler_params=_params(("parallel", "parallel")),
        name="adaln_mod",
    )(cvec, w_mod, b_mod.reshape(DEPTH, 1, n))


def _norm_mod_kernel(h_ref, g_ref, sc_ref, sh_ref, o_ref):
    x = h_ref[0]
    y = x * lax.rsqrt(jnp.mean(x * x, axis=-1, keepdims=True) + EPS) * g_ref[...]
    o_ref[0] = (y * (1.0 + sc_ref[0]) + sh_ref[0]).astype(BF16)


def _norm_mod(h, g, sc, sh):
    ts = CTX
    mod_spec = pl.BlockSpec((1, 1, D), lambda b, t: (jnp.where(t == 0, NB, b), 0, 0))
    return pl.pallas_call(
        _norm_mod_kernel,
        grid=(NB, S // ts),
        in_specs=[
            pl.BlockSpec((1, ts, D), lambda b, t: (b, t, 0)),
            pl.BlockSpec((1, D), lambda b, t: (0, 0)),
            mod_spec,
            mod_spec,
        ],
        out_specs=pl.BlockSpec((1, ts, D), lambda b, t: (b, t, 0)),
        out_shape=jax.ShapeDtypeStruct((NB, S, D), BF16),
        compiler_params=_params(("parallel", "parallel")),
        name="norm_mod",
    )(h, g.reshape(1, D), sc, sh)


def _xw(x_ref, w_ref):
    return _dot(x_ref[0], w_ref[0].astype(BF16))


def _proj_call(kernel, x, w, layer, col_blk0, n_out, tn, extra, extra_specs, out_dtype, name,
               w_blk_offsets=(0,), scratch=(), x_single=False):
    k = x.shape[-1]
    w_specs = [pl.BlockSpec((1, k, tn), functools.partial(lambda b, j, off: (layer, 0, col_blk0 + off + j), off=off))
               for off in w_blk_offsets]
    x_mode = {"pipeline_mode": pl.Buffered(1)} if x_single else {}
    return pl.pallas_call(
        kernel,
        grid=(NB, n_out // tn),
        in_specs=[pl.BlockSpec((1, S, k), lambda b, j: (b, 0, 0), **x_mode)] + w_specs + extra_specs,
        out_specs=pl.BlockSpec((1, S, tn), lambda b, j: (b, 0, j)),
        out_shape=jax.ShapeDtypeStruct((NB, S, n_out), out_dtype),
        scratch_shapes=list(scratch),
        compiler_params=_params(("parallel", "parallel")),
        name=name,
    )(x, *([w] * len(w_blk_offsets)), *extra)


def _ffn_up_kernel(x_ref, wg_ref, wv_ref, cw_ref, cb_ref, o_ref, *, tn, sub):
    x = x_ref[0]
    tiles = [slice(s * tn, (s + 1) * tn) for s in range(sub)]
    gates = [_dot(x, wg_ref[0, :, cols].astype(BF16)) for cols in tiles]
    vals = [_dot(x, wv_ref[0, :, cols].astype(BF16)) for cols in tiles]
    for cols, gate, val in zip(tiles, gates, vals):
        conv = _seg_conv(gate, cw_ref[0, :, cols]) + cb_ref[0, :, cols]
        o_ref[0, :, cols] = (_silu(conv) * val).astype(BF16)


def _ffn_up(n, w_up, conv_w, conv_b, layer):
    tn, sub = 256, 2
    tb = tn * sub
    nblk = D_FF // tb
    extra_specs = [
        pl.BlockSpec((1, FFN_CONV_K, tb), lambda b, j: (layer, 0, j)),
        pl.BlockSpec((1, 1, tb), lambda b, j: (layer, 0, j)),
    ]
    return _proj_call(functools.partial(_ffn_up_kernel, tn=tn, sub=sub), n, w_up, layer, 0, D_FF, tb,
                      (conv_w, conv_b.reshape(DEPTH, 1, D_FF)), extra_specs, BF16, "ffn_up",
                      w_blk_offsets=(0, nblk), x_single=True)


def _qk_kernel(x_ref, w_ref, g_ref, cos_ref, sa_ref, sb_ref, o_ref):
    y = _xw(x_ref, w_ref)
    g = g_ref[0]
    cos, sa, sb = cos_ref[...], sa_ref[...], sb_ref[...]
    for hh in range(y.shape[1] // HEAD_DIM):
        t = y[:, hh * HEAD_DIM:(hh + 1) * HEAD_DIM]
        t = t * lax.rsqrt(jnp.mean(t * t, axis=-1, keepdims=True) + EPS) * g
        r = t * cos + pltpu.roll(t, 96, 1) * sa + pltpu.roll(t, 32, 1) * sb
        o_ref[0, :, hh * HEAD_DIM:(hh + 1) * HEAD_DIM] = r.astype(BF16)


def _plain_kernel(x_ref, w_ref, o_ref):
    o_ref[0] = _xw(x_ref, w_ref).astype(o_ref.dtype)


def _glu_kernel(x_ref, wa_ref, wb_ref, o_ref):
    a = _xw(x_ref, wa_ref)
    b = _xw(x_ref, wb_ref)
    o_ref[0] = (a * jax.nn.sigmoid(b)).astype(BF16)


def _rope_tables():
    row = jnp.repeat(jnp.arange(SEQ // GRID_W, dtype=F32), GRID_W)
    col = jnp.tile(jnp.arange(GRID_W, dtype=F32), SEQ // GRID_W)
    half = HEAD_DIM // 2
    inv_freq = ROPE_THETA ** (-jnp.arange(0, half, 2, dtype=F32) / half)
    ang_r = row[:, None] * inv_freq
    ang_c = col[:, None] * inv_freq
    ang = jnp.concatenate([ang_r, ang_r, ang_c, ang_c], axis=-1)
    cos, sin = jnp.cos(ang), jnp.sin(ang)
    quarter = (jnp.arange(HEAD_DIM) // (HEAD_DIM // 4)) % 2
    sa = jnp.where(quarter == 0, -sin, 0.0)
    sb = jnp.where(quarter == 1, sin, 0.0)
    pad1 = jnp.ones((CTX, HEAD_DIM), F32)
    pad0 = jnp.zeros((CTX, HEAD_DIM), F32)
    return (jnp.concatenate([pad1, cos]), jnp.concatenate([pad0, sa]), jnp.concatenate([pad0, sb]))


def _even_proj(n, w_in, q_g, k_g, rope, i):
    tn = 256
    gains = jnp.stack([q_g, k_g]).reshape(2, 1, HEAD_DIM)
    tab_spec = pl.BlockSpec((S, HEAD_DIM), lambda b, j: (0, 0))
    qk = _proj_call(_qk_kernel, n, w_in, i, 0, A_Q_W + A_KV_W, tn, (gains,) + rope,
                    [pl.BlockSpec((1, 1, HEAD_DIM), lambda b, j: (jnp.where(j < A_Q_W // tn, 0, 1), 0, 0)),
                     tab_spec, tab_spec, tab_spec], BF16, "even_qk")
    v = _proj_call(_plain_kernel, n, w_in, i, (A_Q_W + A_KV_W) // tn, A_KV_W, tn, (), [], BF16, "even_v")
    glu0 = (A_Q_W + 2 * A_KV_W) // tn
    glu = _proj_call(_glu_kernel, n, w_in, i, glu0, CONV_CH, tn, (), [], BF16, "even_glu",
                     w_blk_offsets=(0, CONV_CH // tn))
    return qk, v, glu


def _mm_res_kernel(*refs, n_x, k_sizes, tm):
    x_refs = refs[:n_x]
    w_ref, h_ref, g_ref, o_ref = refs[n_x:]
    i = pl.program_id(0)
    acc = None
    off = 0
    for x_ref, kx in zip(x_refs, k_sizes):
        part = _dot(x_ref[...], w_ref[0, off:off + kx, :].astype(BF16))
        acc = part if acc is None else acc + part
        off += kx
    tiles_per_sample = S // tm
    b = i // tiles_per_sample
    row = lax.broadcasted_iota(jnp.int32, (tm, 1), 0) + (i % tiles_per_sample) * tm
    gate = jnp.where(row < CTX, g_ref[NB:NB + 1, :], g_ref[pl.ds(b, 1), :])
    o_ref[...] = h_ref[...] + gate * acc


def _mm_res(xs, w, layer, h, gate, name):
    tn, tm = 256, S // 2
    k_sizes = tuple(x.shape[-1] for x in xs)
    k = sum(k_sizes)
    m = NB * S
    kern = functools.partial(_mm_res_kernel, n_x=len(xs), k_sizes=k_sizes, tm=tm)
    return pl.pallas_call(
        kern,
        grid=(m // tm, D // tn),
        in_specs=[pl.BlockSpec((tm, kx), lambda i, j: (i, 0)) for kx in k_sizes] + [
            pl.BlockSpec((1, k, tn), lambda i, j: (layer, 0, j)),
            pl.BlockSpec((tm, tn), lambda i, j: (i, j)),
            pl.BlockSpec((8, tn), lambda i, j: (0, j)),
        ],
        out_specs=pl.BlockSpec((tm, tn), lambda i, j: (i, j)),
        out_shape=jax.ShapeDtypeStruct((m, D), F32),
        compiler_params=_params(("parallel", "parallel")),
        name=name,
    )(*xs, w, h, gate)


_CPAD = 16
_CROWS = _CPAD + CTX + _CPAD + SEQ + _CPAD
_CBLK = 256


def _conformer_kernel(x_ref, w_ref, b_ref, g_ref, beta_ref, o_ref, pad_s):
    zeros = jnp.zeros((_CPAD, 128), F32)
    pad_s[0:_CPAD, :] = zeros
    pad_s[_CPAD + CTX:2 * _CPAD + CTX, :] = zeros
    pad_s[_CROWS - _CPAD:_CROWS, :] = zeros
    pad_s[_CPAD:_CPAD + CTX, :] = x_ref[0, 0:CTX, :].astype(F32)
    pad_s[2 * _CPAD + CTX:2 * _CPAD + S, :] = x_ref[0, CTX:S, :].astype(F32)
    w = w_ref[0]
    left = (CONV_K - 1) // 2
    for blk in range(S // _CBLK):
        r0 = blk * _CBLK
        p0 = r0 + (_CPAD if r0 < CTX else 2 * _CPAD)
        acc = jnp.zeros((_CBLK, 128), F32)
        for k in range(CONV_K):
            acc = acc + w[k:k + 1, :] * pad_s[p0 + k - left:p0 + k - left + _CBLK, :]
        hcv = acc + b_ref[0]
        mu = jnp.mean(hcv, axis=-1, keepdims=True)
        dlt = hcv - mu
        var = jnp.mean(dlt * dlt, axis=-1, keepdims=True)
        hn = dlt * lax.rsqrt(var + EPS) * g_ref[0] + beta_ref[0]
        o_ref[0, r0:r0 + _CBLK, :] = _silu(hn).astype(BF16)


def _conformer(glu, dw_w, dw_b, ln_g, ln_b, i):
    groups = CONV_CH // 128
    vec = lambda a: a.reshape(a.shape[0], 1, CONV_CH)
    vspec = pl.BlockSpec((1, 1, 128), lambda b, c: (i, 0, c))
    return pl.pallas_call(
        _conformer_kernel,
        grid=(NB, groups),
        in_specs=[
            pl.BlockSpec((1, S, 128), lambda b, c: (b, 0, c)),
            pl.BlockSpec((1, CONV_K, 128), lambda b, c: (i, 0, c)),
            vspec, vspec, vspec,
        ],
        out_specs=pl.BlockSpec((1, S, 128), lambda b, c: (b, 0, c)),
        out_shape=jax.ShapeDtypeStruct((NB, S, CONV_CH), BF16),
        scratch_shapes=[pltpu.VMEM((_CROWS, 128), F32)],
        compiler_params=_params(("parallel", "parallel")),
        name="conformer_conv",
    )(glu, dw_w, vec(dw_b), vec(ln_g), vec(ln_b))


def _attn_kernel(sink_ref, q_ref, kc_ref, vc_ref, *rest, band, layer_row):
    if band:
        kp_ref, kn_ref, kx_ref, vp_ref, vn_ref, vx_ref, mask_ref, o_ref = rest
    else:
        (o_ref,) = rest
    h = pl.program_id(1)
    nq = q_ref.shape[1]
    q = jnp.concatenate([q_ref[0, :, g * HEAD_DIM:(g + 1) * HEAD_DIM] for g in range(A_GROUP)], axis=0)
    scale = HEAD_DIM ** -0.5
    if band:
        kcat = jnp.concatenate([kc_ref[0], kp_ref[0], kn_ref[0], kx_ref[0]], axis=0)
        vcat = jnp.concatenate([vc_ref[0], vp_ref[0], vn_ref[0], vx_ref[0]], axis=0)
    else:
        kcat, vcat = kc_ref[0], vc_ref[0]
    s = _dot_nt(q, kcat) * scale
    rows = A_GROUP * nq
    if band:
        s = s + mask_ref[0]
    gidx = lax.broadcasted_iota(jnp.int32, (rows, 1), 0) // nq
    sink = jnp.zeros((rows, 1), F32)
    for g in range(A_GROUP):
        sink = jnp.where(gidx == g, sink_ref[layer_row, h * A_GROUP + g], sink)
    m = jnp.maximum(jnp.max(s, axis=-1, keepdims=True), sink)
    p = jnp.exp(s - m)
    den = jnp.sum(p, axis=-1, keepdims=True) + jnp.exp(sink - m)
    o = _dot(p.astype(BF16), vcat) / den
    for g in range(A_GROUP):
        o_ref[0, :, g * HEAD_DIM:(g + 1) * HEAD_DIM] = o[g * nq:(g + 1) * nq].astype(BF16)


def _attention(qk, v, sink, i):
    gw = A_GROUP * HEAD_DIM
    kcol0 = A_Q_W // HEAD_DIM
    cb = CTX // ABLK
    nb = SEQ // ABLK
    smem = pl.BlockSpec(memory_space=pltpu.SMEM)

    def blk(rows, col_fn, row_fn):
        return pl.BlockSpec((1, rows, HEAD_DIM), lambda b, h, n: (b, row_fn(n), col_fn(h)))

    kcol = lambda h: kcol0 + h
    vcol = lambda h: h
    prev_r = lambda n: cb + jnp.maximum(n - 1, 0)
    cur_r = lambda n: cb + n
    next_r = lambda n: cb + jnp.minimum(n + 1, nb - 1)
    zero_r = lambda n: 0
    r = jnp.arange(A_GROUP * ABLK)[:, None] % ABLK
    c = jnp.arange(CTX + 3 * ABLK)[None, :] - CTX
    in_window = jnp.abs(c - ABLK - r) <= WINDOW
    ok = [(c < 0) | (in_window & keep) for keep in (c >= ABLK, c >= 0, c < 2 * ABLK)]
    masks = jnp.where(jnp.stack(ok), 0.0, -jnp.inf).astype(F32)
    mask_spec = pl.BlockSpec((1, A_GROUP * ABLK, CTX + 3 * ABLK),
                             lambda b, h, n: (jnp.where(n == 0, 0, jnp.where(n == nb - 1, 2, 1)), 0, 0))
    lat = pl.pallas_call(
        functools.partial(_attn_kernel, band=True, layer_row=i),
        grid=(NB, A_KV_HEADS, nb),
        in_specs=[
            smem,
            pl.BlockSpec((1, ABLK, gw), lambda b, h, n: (b, cb + n, h)),
            blk(CTX, kcol, zero_r), blk(CTX, vcol, zero_r),
            blk(ABLK, kcol, prev_r), blk(ABLK, kcol, cur_r), blk(ABLK, kcol, next_r),
            blk(ABLK, vcol, prev_r), blk(ABLK, vcol, cur_r), blk(ABLK, vcol, next_r),
            mask_spec,
        ],
        out_specs=pl.BlockSpec((1, ABLK, gw), lambda b, h, n: (b, n, h)),
        out_shape=jax.ShapeDtypeStruct((NB, SEQ, A_Q_W), BF16),
        compiler_params=_params(("parallel", "parallel", "parallel")),
        name="attn_latent",
    )(sink, qk, qk, v, qk, qk, qk, v, v, v, masks)
    ctx = pl.pallas_call(
        functools.partial(_attn_kernel, band=False, layer_row=i),
        grid=(NB, A_KV_HEADS),
        in_specs=[
            smem,
            pl.BlockSpec((1, CTX, gw), lambda b, h: (b, 0, h)),
            pl.BlockSpec((1, CTX, HEAD_DIM), lambda b, h: (b, 0, kcol0 + h)),
            pl.BlockSpec((1, CTX, HEAD_DIM), lambda b, h: (b, 0, h)),
        ],
        out_specs=pl.BlockSpec((1, CTX, gw), lambda b, h: (b, 0, h)),
        out_shape=jax.ShapeDtypeStruct((NB, CTX, A_Q_W), BF16),
        compiler_params=_params(("parallel", "parallel")),
        name="attn_context",
    )(sink, qk, qk, v)
    return jnp.concatenate([ctx, lat], axis=1)


def _dn_conv_kernel(x_ref, w_ref, cw_ref, o_ref, *, tn, sub, l2norm):
    j = pl.program_id(1)
    x = x_ref[0]
    for s in range(sub):
        y = _dot(x, w_ref[0, :, s * tn:(s + 1) * tn].astype(BF16))
        a = _silu(_seg_conv(y, cw_ref[0, :, s * tn:(s + 1) * tn]))
        if not l2norm:
            o_ref[0, :, s * tn:(s + 1) * tn] = a.astype(BF16)
            continue
        qscale = jnp.where((j * sub + s) * tn < DN_QK_W, DN_DK ** -0.5, 1.0)
        for hh in range(tn // DN_DK):
            c0 = s * tn + hh * DN_DK
            t = a[:, hh * DN_DK:(hh + 1) * DN_DK]
            inv = lax.rsqrt(jnp.sum(t * t, axis=-1, keepdims=True) + EPS) * qscale
            o_ref[0, :, c0:c0 + DN_DK] = (t * inv).astype(BF16)


_GATE_ROWS = 4 * CHUNK


def _dn_gate_kernel(x_ref, w_ref, alog_ref, dtb_ref, o_ref, w_s):
    @pl.when((pl.program_id(0) == 0) & (pl.program_id(1) == 0))
    def _():
        w_s[...] = w_ref[0].astype(BF16)

    ba = _dot(x_ref[0], w_s[...])
    beta = jax.nn.sigmoid(ba)
    g = -jnp.exp(alog_ref[...]) * jax.nn.softplus(ba + dtb_ref[...])
    ii = lax.broadcasted_iota(jnp.int32, (_GATE_ROWS, _GATE_ROWS), 0)
    jj = lax.broadcasted_iota(jnp.int32, (_GATE_ROWS, _GATE_ROWS), 1)
    same_chunk = (ii // CHUNK) == (jj // CHUNK)
    tri_f = jnp.where(same_chunk & (ii >= jj), 1.0, 0.0).astype(BF16)
    tri_b = jnp.where(same_chunk & (ii <= jj), 1.0, 0.0).astype(BF16)
    g1 = g.astype(BF16)
    r1 = g - g1.astype(F32)
    g2 = r1.astype(BF16)
    g3 = (r1 - g2.astype(F32)).astype(BF16)
    cs_f = _dot(tri_f, g1) + _dot(tri_f, g2) + _dot(tri_f, g3)
    cs_b = _dot(tri_b, g1) + _dot(tri_b, g2) + _dot(tri_b, g3)
    col = lax.broadcasted_iota(jnp.int32, (_GATE_ROWS, 128), 1)
    o_ref[0] = jnp.where(col < 2 * DN_V_HEADS, beta, jnp.where(col < 3 * DN_V_HEADS, cs_f, cs_b))


def _dn_project(n, w_in, conv_w, a_log, dt_bias, i):
    tn, sub = 256, 2
    tb = tn * sub

    def conv_proj(col0, width, l2norm, name):
        blk0 = col0 // tb
        return _proj_call(functools.partial(_dn_conv_kernel, tn=tn, sub=sub, l2norm=l2norm), n, w_in, i, blk0,
                          width, tb, (conv_w,),
                          [pl.BlockSpec((1, DN_SHORT_K, tb), lambda b, j: (i, 0, blk0 + j))], BF16, name)

    qk = conv_proj(0, 2 * DN_QK_W, True, "dn_qk")
    vv = conv_proj(2 * DN_QK_W, DN_V_W, False, "dn_v")
    z = _proj_call(_plain_kernel, n, w_in, i, DN_CONV_CH // tn, DN_V_W, tn, (), [], BF16, "dn_z")
    zeros = jnp.zeros((2 * DN_V_HEADS,), F32)
    alog = jnp.concatenate([zeros, a_log[i].reshape(-1)]).reshape(1, 128)
    dtb = jnp.concatenate([zeros, dt_bias[i].reshape(-1)]).reshape(1, 128)
    gates = pl.pallas_call(
        _dn_gate_kernel,
        grid=(NB, S // _GATE_ROWS),
        in_specs=[
            pl.BlockSpec((1, _GATE_ROWS, D), lambda b, c: (b, c, 0)),
            pl.BlockSpec((1, D, 128), lambda b, c: (i, 0, (DN_CONV_CH + DN_V_W) // 128)),
            pl.BlockSpec((1, 128), lambda b, c: (0, 0)),
            pl.BlockSpec((1, 128), lambda b, c: (0, 0)),
        ],
        out_specs=pl.BlockSpec((1, _GATE_ROWS, 128), lambda b, c: (b, c, 0)),
        out_shape=jax.ShapeDtypeStruct((NB, S, 128), F32),
        scratch_shapes=[pltpu.VMEM((D, 128), BF16)],
        compiler_params=_params(("arbitrary", "arbitrary")),
        name="dn_gates",
    )(n, w_in, alog, dtb)
    return qk, vv, z, gates


_DN_G = 32
_DN_HG = DN_V_HEADS // _DN_G
_DN_REP = DN_V_HEADS // DN_K_HEADS


def _bmm(a, b):
    return lax.dot_general(a, b, (((2,), (1,)), ((0,), (0,))), preferred_element_type=F32)


def _bmm_nt(a, b):
    return lax.dot_general(a, b, (((2,), (2,)), ((0,), (0,))), preferred_element_type=F32)


def _bmm_tn(a, b):
    return lax.dot_general(a, b, (((1,), (1,)), ((0,), (0,))), preferred_element_type=F32)


def _hi_lo(x):
    hi = x.astype(BF16).astype(F32)
    return hi, x - hi


def _bmm_f32_lhs(a, b):
    ah, al = _hi_lo(a)
    lhs = jnp.concatenate([ah, al], axis=2).astype(BF16)
    return _bmm(lhs, jnp.concatenate([b, b], axis=1))


def _bmm_f32(a, b):
    ah, al = _hi_lo(a)
    bh, bl = _hi_lo(b)
    lhs = jnp.concatenate([ah, al, ah], axis=2).astype(BF16)
    rhs = jnp.concatenate([bh, bh, bl], axis=1).astype(BF16)
    return _bmm(lhs, rhs)


def _dn_scan_kernel(q_ref, k_ref, v_ref, g1_ref, g2_ref, o_ref, s_ref):
    d = pl.program_id(1)
    step = pl.program_id(3)
    ng, nk = _DN_G, _DN_G // _DN_REP

    @pl.when(step == 0)
    def _():
        s_ref[...] = jnp.zeros_like(s_ref)

    fwd = d == 0
    ii = lax.broadcasted_iota(jnp.int32, (CHUNK, CHUNK), 0)
    jj = lax.broadcasted_iota(jnp.int32, (CHUNK, CHUNK), 1)
    later = (ii - jj) * jnp.where(fwd, 1, -1)
    incl = later >= 0
    strict = jnp.where(later > 0, 1.0, 0.0)
    eye = jnp.where(ii == jj, 1.0, 0.0)

    def same_block(size):
        return jnp.where((ii // size) == (jj // size), 1.0, 0.0)

    g1 = g1_ref[0, 0, 0]
    g2 = g2_ref[0, 0, 0, 0]
    q3 = jnp.stack([q_ref[0, :, i * DN_DK:(i + 1) * DN_DK] for i in range(nk)])
    k3 = jnp.stack([k_ref[0, :, i * DN_DK:(i + 1) * DN_DK] for i in range(nk)])
    v3 = jnp.stack([v_ref[0, :, e * DN_DV:(e + 1) * DN_DV] for e in range(ng)])
    beta_r = jnp.stack([g2[e:e + 1, :] for e in range(ng)])
    gc_r = jnp.stack([g2[ng + e:ng + e + 1, :] for e in range(ng)])
    beta_c = jnp.stack([jnp.broadcast_to(g1[:, e:e + 1], (CHUNK, CHUNK)) for e in range(ng)])
    gc_c = jnp.stack([jnp.broadcast_to(g1[:, ng + e:ng + e + 1], (CHUNK, DN_DK)) for e in range(ng)])
    g_last = jnp.where(fwd, gc_r[:, :, CHUNK - 1:CHUNK], gc_r[:, :, 0:1])

    decay = jnp.exp(jnp.where(incl, gc_c[:, :, :CHUNK] - gc_r, -jnp.inf))
    kk = jnp.repeat(_bmm_nt(k3, k3), _DN_REP, axis=0)
    qk = jnp.repeat(_bmm_nt(q3, k3), _DN_REP, axis=0)
    a = kk * (beta_c * decay) * strict

    def mm(x, y):
        return _bmm(x.astype(BF16), y.astype(BF16))

    d4 = a * same_block(4)
    imd = eye - d4
    t = imd + mm(imd, mm(d4, d4))
    size = 4
    while size < CHUNK:
        n_s = a * (same_block(2 * size) - same_block(size))
        t = t - mm(t, mm(n_s, t))
        size *= 2
    t = t + mm(t, eye - t - _bmm_f32(a, t))

    k_v = jnp.repeat(k3, _DN_REP, axis=0)
    tb = t * beta_r
    u = _bmm_f32_lhs(tb, v3)
    w = _bmm_f32_lhs(tb * jnp.exp(gc_r), k_v)

    st = s_ref[...]
    stb = st.astype(BF16)
    qg = jnp.repeat(q3, _DN_REP, axis=0) * jnp.exp(gc_c).astype(BF16)
    ws = _bmm(jnp.concatenate([w.astype(BF16), qg], axis=1), stb)
    v_new = u - ws[:, :CHUNK]
    vnb = v_new.astype(BF16)
    o = ws[:, CHUNK:] + _bmm((qk * decay).astype(BF16), vnb)
    for e in range(ng):
        o_ref[0, 0, :, e * DN_DV:(e + 1) * DN_DV] = o[e].astype(o_ref.dtype)
    kd = k_v * jnp.exp(g_last - gc_c).astype(BF16)
    s_ref[...] = st * jnp.exp(g_last) + _bmm_tn(kd, vnb)


def _dn_scan(qk, vv, gates):
    hpg = DN_V_HEADS // _DN_HG
    kw = (DN_K_HEADS // _DN_HG) * DN_DK
    vw = hpg * DN_DV
    bt = gates[..., :2 * DN_V_HEADS].reshape(NB, S, 2, _DN_HG, hpg)
    gc = gates[..., 2 * DN_V_HEADS:].reshape(NB, S, 2, _DN_HG, hpg)
    g1 = jnp.transpose(jnp.concatenate([bt, gc], axis=-1), (0, 2, 3, 1, 4))
    g2 = jnp.swapaxes(g1.reshape(NB, 2, _DN_HG, NCH, CHUNK, 2 * hpg), -1, -2)

    def chunk(d, s):
        return jnp.where(d == 0, s, jnp.where(s < CTX_CH, CTX_CH - 1 - s, NCH + CTX_CH - 1 - s))

    return pl.pallas_call(
        _dn_scan_kernel,
        grid=(NB, 2, _DN_HG, NCH),
        in_specs=[
            pl.BlockSpec((1, CHUNK, kw), lambda b, d, g, s: (b, chunk(d, s), g)),
            pl.BlockSpec((1, CHUNK, kw), lambda b, d, g, s: (b, chunk(d, s), DN_QK_W // kw + g)),
            pl.BlockSpec((1, CHUNK, vw), lambda b, d, g, s: (b, chunk(d, s), g)),
            pl.BlockSpec((1, 1, 1, CHUNK, 2 * hpg), lambda b, d, g, s: (b, d, g, chunk(d, s), 0)),
            pl.BlockSpec((1, 1, 1, 1, 2 * hpg, CHUNK), lambda b, d, g, s: (b, d, g, chunk(d, s), 0, 0)),
        ],
        out_specs=pl.BlockSpec((1, 1, CHUNK, vw), lambda b, d, g, s: (b, d, chunk(d, s), g)),
        out_shape=jax.ShapeDtypeStruct((NB, 2, S, DN_V_W), BF16),
        scratch_shapes=[pltpu.VMEM((hpg, DN_DK, DN_DV), F32)],
        compiler_params=_params(("parallel", "parallel", "parallel", "arbitrary")),
        name="dn_scan",
    )(qk, qk, vv, g1, g2)


def _dn_out_kernel(o_ref, z_ref, g_ref, y_ref):
    g = g_ref[...]
    for hh in range(o_ref.shape[-1] // DN_DV):
        sl = slice(hh * DN_DV, (hh + 1) * DN_DV)
        o = o_ref[0, 0, :, sl].astype(F32) + o_ref[0, 1, :, sl].astype(F32)
        o = o * lax.rsqrt(jnp.mean(o * o, axis=-1, keepdims=True) + EPS) * g
        y_ref[0, :, sl] = (o * _silu(z_ref[0, :, sl].astype(F32))).astype(BF16)


def _dn_gated_norm(o, z, norm_g):
    ts, tw = S // 3, 1024
    return pl.pallas_call(
        _dn_out_kernel,
        grid=(NB, S // ts, DN_V_W // tw),
        in_specs=[
            pl.BlockSpec((1, 2, ts, tw), lambda b, t, c: (b, 0, t, c)),
            pl.BlockSpec((1, ts, tw), lambda b, t, c: (b, t, c)),
            pl.BlockSpec((1, DN_DV), lambda b, t, c: (0, 0)),
        ],
        out_specs=pl.BlockSpec((1, ts, tw), lambda b, t, c: (b, t, c)),
        out_shape=jax.ShapeDtypeStruct((NB, S, DN_V_W), BF16),
        compiler_params=_params(("parallel", "parallel", "parallel")),
        name="dn_gated_norm",
    )(o, z, norm_g.reshape(1, DN_DV))


def kernel(x, c, ctx, c_ctx, w_mod, b_mod, norm1_g, norm2_g, ffn_w_up, ffn_conv_w, ffn_conv_b, ffn_w_down, even_w_in, even_w_out, attn_q_norm_g, attn_k_norm_g, attn_sink, conv_dw_w, conv_dw_b, conv_ln_g, conv_ln_b, dn_w_in, dn_conv_w, dn_a_log, dn_dt_bias, dn_norm_g, dn_w_out):
    cvec = jnp.concatenate([c, c_ctx[None, :], jnp.zeros((8 - NB - 1, D), F32)], axis=0)
    mods = _modulation(cvec, w_mod, b_mod).reshape(DEPTH, 8, N_MOD, D)
    rope = _rope_tables()
    h = jnp.concatenate([ctx, x], axis=1)
    m = NB * S
    for layer in range(DEPTH):
        i = layer // 2
        sh1, sc1, g1, sh2, sc2, g2 = [mods[layer, :, j, :] for j in range(N_MOD)]
        n = _norm_mod(h, norm1_g[layer], sc1.reshape(8, 1, D), sh1.reshape(8, 1, D))
        hf = h.reshape(m, D)
        if layer % 2 == 0:
            qk, v, glu = _even_proj(n, even_w_in, attn_q_norm_g[i], attn_k_norm_g[i], rope, i)
            att = _attention(qk, v, attn_sink, i)
            cv = _conformer(glu, conv_dw_w, conv_dw_b, conv_ln_g, conv_ln_b, i)
            hf = _mm_res([att.reshape(m, A_Q_W), cv.reshape(m, CONV_CH)], even_w_out, i, hf, g1, "even_out")
        else:
            qk, vv, z, gates = _dn_project(n, dn_w_in, dn_conv_w, dn_a_log, dn_dt_bias, i)
            o = _dn_scan(qk, vv, gates)
            y = _dn_gated_norm(o, z, dn_norm_g[i])
            hf = _mm_res([y.reshape(m, DN_V_W)], dn_w_out, i, hf, g1, "dn_out")
        h = hf.reshape(NB, S, D)
        n2 = _norm_mod(h, norm2_g[layer], sc2.reshape(8, 1, D), sh2.reshape(8, 1, D))
        f = _ffn_up(n2, ffn_w_up, ffn_conv_w, ffn_conv_b, layer)
        h = _mm_res([f.reshape(m, D_FF)], ffn_w_down, layer, h.reshape(m, D), g2, "ffn_down").reshape(NB, S, D)
    return h[:, CTX:, :]
```

```python
import functools
import math

import jax
import jax.numpy as jnp
from jax import lax
from jax.experimental import pallas as pl
from jax.experimental.pallas import tpu as pltpu

F32 = jnp.float32
BF16 = jnp.bfloat16

D = 2048
NB = 4
SEQ = 2048
CTX = 256
S = CTX + SEQ
DEPTH = 4
GRID_W = 64
EPS = 1e-6
N_MOD = 6

HEAD_DIM = 128
A_Q_HEADS = 8
A_KV_HEADS = 2
A_GROUP = 4
A_Q_W = 1024
A_KV_W = 256
WINDOW = 128
ABLK = 128
ROPE_THETA = 10000.0
CONV_CH = 1024
CONV_K = 31
EVEN_IN_W = 3584

DN_DK = 128
DN_DV = 128
DN_K_HEADS = 16
DN_V_HEADS = 32
DN_QK_W = 2048
DN_V_W = 4096
DN_CONV_CH = 8192
DN_BA_W = 128
DN_SHORT_K = 5
CHUNK = 64
NCH = S // CHUNK
CTX_CH = CTX // CHUNK

D_FF = 5632
FFN_CONV_K = 3

VMEM_LIMIT = 58 * 1024 * 1024


def _params(sem, vmem=VMEM_LIMIT):
    return pltpu.CompilerParams(dimension_semantics=sem, vmem_limit_bytes=vmem)


def _silu(x):
    return x * jax.nn.sigmoid(x)


def _dot(a, b):
    return jnp.dot(a, b, preferred_element_type=F32)


def _dot_nt(a, b):
    return lax.dot_general(a, b, (((1,), (1,)), ((), ())), preferred_element_type=F32)


def _dot_tn(a, b):
    return lax.dot_general(a, b, (((0,), (0,)), ((), ())), preferred_element_type=F32)


_SUB = 8


def _seg_conv(y, cw):
    taps, n = cw.shape
    left = (taps - 1) // 2
    assert left < _SUB
    y3 = y.reshape(S // _SUB, _SUB, n)
    sub = lax.broadcasted_iota(jnp.int32, (1, _SUB, n), 1)
    nc = CTX // _SUB
    zero = jnp.zeros((1, _SUB, n), F32)
    acc = cw[left:left + 1] * y3
    for k in range(taps):
        s = k - left
        if s == 0:
            continue
        r = pltpu.roll(y3, (-s) % _SUB, 1)
        if s > 0:
            other = jnp.concatenate([r[1:nc], zero, r[nc + 1:], zero], axis=0)
            shifted = jnp.where(sub < _SUB - s, r, other)
        else:
            other = jnp.concatenate([zero, r[:nc - 1], zero, r[nc:-1]], axis=0)
            shifted = jnp.where(sub >= -s, r, other)
        acc = acc + cw[k:k + 1] * shifted
    return acc.reshape(S, n)


def _mod_kernel(c_ref, w_ref, b_ref, o_ref):
    a = _silu(c_ref[...]).astype(BF16)
    o_ref[0] = _dot(a, w_ref[0].astype(BF16)) + b_ref[0]


def _modulation(cvec, w_mod, b_mod):
    tn = 1024
    n = N_MOD * D
    return pl.pallas_call(
        _mod_kernel,
        grid=(DEPTH, n // tn),
        in_specs=[
            pl.BlockSpec((8, D), lambda l, j: (0, 0)),
            pl.BlockSpec((1, D, tn), lambda l, j: (l, 0, j)),
            pl.BlockSpec((1, 1, tn), lambda l, j: (l, 0, j)),
        ],
        out_specs=pl.BlockSpec((1, 8, tn), lambda l, j: (l, 0, j)),
        out_shape=jax.ShapeDtypeStruct((DEPTH, 8, n), F32),
        compiler_params=_params(("parallel", "parallel")),
        name="adaln_mod",
    )(cvec, w_mod, b_mod.reshape(DEPTH, 1, n))


def _norm_mod_kernel(h_ref, g_ref, sc_ref, sh_ref, o_ref):
    x = h_ref[0]
    y = x * lax.rsqrt(jnp.mean(x * x, axis=-1, keepdims=True) + EPS) * g_ref[...]
    o_ref[0] = (y * (1.0 + sc_ref[0]) + sh_ref[0]).astype(BF16)


def _norm_mod(h, g, sc, sh):
    ts = CTX
    mod_spec = pl.BlockSpec((1, 1, D), lambda b, t: (jnp.where(t == 0, NB, b), 0, 0))
    return pl.pallas_call(
        _norm_mod_kernel,
        grid=(NB, S // ts),
        in_specs=[
            pl.BlockSpec((1, ts, D), lambda b, t: (b, t, 0)),
            pl.BlockSpec((1, D), lambda b, t: (0, 0)),
            mod_spec,
            mod_spec,
        ],
        out_specs=pl.BlockSpec((1, ts, D), lambda b, t: (b, t, 0)),
        out_shape=jax.ShapeDtypeStruct((NB, S, D), BF16),
        compiler_params=_params(("parallel", "parallel")),
        name="norm_mod",
    )(h, g.reshape(1, D), sc, sh)


def _xw(x_ref, w_ref):
    return _dot(x_ref[0], w_ref[0].astype(BF16))


def _proj_call(kernel, x, w, layer, col_blk0, n_out, tn, extra, extra_specs, out_dtype, name,
               w_blk_offsets=(0,), scratch=(), x_single=False):
    k = x.shape[-1]
    w_specs = [pl.BlockSpec((1, k, tn), functools.partial(lambda b, j, off: (layer, 0, col_blk0 + off + j), off=off))
               for off in w_blk_offsets]
    x_mode = {"pipeline_mode": pl.Buffered(1)} if x_single else {}
    return pl.pallas_call(
        kernel,
        grid=(NB, n_out // tn),
        in_specs=[pl.BlockSpec((1, S, k), lambda b, j: (b, 0, 0), **x_mode)] + w_specs + extra_specs,
        out_specs=pl.BlockSpec((1, S, tn), lambda b, j: (b, 0, j)),
        out_shape=jax.ShapeDtypeStruct((NB, S, n_out), out_dtype),
        scratch_shapes=list(scratch),
        compiler_params=_params(("parallel", "parallel")),
        name=name,
    )(x, *([w] * len(w_blk_offsets)), *extra)


def _ffn_up_kernel(x_ref, wg_ref, wv_ref, cw_ref, cb_ref, o_ref, *, tn, sub):
    x = x_ref[0]
    tiles = [slice(s * tn, (s + 1) * tn) for s in range(sub)]
    gates = [_dot(x, wg_ref[0, :, cols].astype(BF16)) for cols in tiles]
    vals = [_dot(x, wv_ref[0, :, cols].astype(BF16)) for cols in tiles]
    for cols, gate, val in zip(tiles, gates, vals):
        conv = _seg_conv(gate, cw_ref[0, :, cols]) + cb_ref[0, :, cols]
        o_ref[0, :, cols] = (_silu(conv) * val).astype(BF16)


def _ffn_up(n, w_up, conv_w, conv_b, layer):
    tn, sub = 256, 2
    tb = tn * sub
    nblk = D_FF // tb
    extra_specs = [
        pl.BlockSpec((1, FFN_CONV_K, tb), lambda b, j: (layer, 0, j)),
        pl.BlockSpec((1, 1, tb), lambda b, j: (layer, 0, j)),
    ]
    return _proj_call(functools.partial(_ffn_up_kernel, tn=tn, sub=sub), n, w_up, layer, 0, D_FF, tb,
                      (conv_w, conv_b.reshape(DEPTH, 1, D_FF)), extra_specs, BF16, "ffn_up",
                      w_blk_offsets=(0, nblk), x_single=True)


def _hi_lo_cols(z):
    hi = z.astype(BF16)
    return jnp.concatenate([hi, (z - hi.astype(F32)).astype(BF16)], axis=1)


def _qk_kernel(x_ref, w_ref, g_ref, cos_ref, sin_ref, mean_ref, rot_ref, o_ref):
    y = _xw(x_ref, w_ref)
    g = jnp.concatenate([g_ref[0]] * (y.shape[1] // HEAD_DIM), axis=1)
    ms = _dot(_hi_lo_cols(y * y), mean_ref[...])
    t = y * lax.rsqrt(ms + EPS) * g
    r = t * cos_ref[...] + _dot(_hi_lo_cols(t), rot_ref[...]) * sin_ref[...]
    o_ref[0] = r.astype(BF16)


def _plain_kernel(x_ref, w_ref, o_ref):
    o_ref[0] = _xw(x_ref, w_ref).astype(o_ref.dtype)


def _glu_kernel(x_ref, wa_ref, wb_ref, o_ref):
    a = _xw(x_ref, wa_ref)
    b = _xw(x_ref, wb_ref)
    o_ref[0] = (a * jax.nn.sigmoid(b)).astype(BF16)


_QK_TN = 256


def _rope_tables():
    row = jnp.repeat(jnp.arange(SEQ // GRID_W, dtype=F32), GRID_W)
    col = jnp.tile(jnp.arange(GRID_W, dtype=F32), SEQ // GRID_W)
    half = HEAD_DIM // 2
    inv_freq = ROPE_THETA ** (-jnp.arange(0, half, 2, dtype=F32) / half)
    ang_r = row[:, None] * inv_freq
    ang_c = col[:, None] * inv_freq
    ang = jnp.concatenate([ang_r, ang_r, ang_c, ang_c], axis=-1)
    reps = _QK_TN // HEAD_DIM
    cos = jnp.tile(jnp.concatenate([jnp.ones((CTX, HEAD_DIM), F32), jnp.cos(ang)]), (1, reps))
    sin = jnp.tile(jnp.concatenate([jnp.zeros((CTX, HEAD_DIM), F32), jnp.sin(ang)]), (1, reps))
    src = jnp.arange(_QK_TN)[:, None]
    dst = jnp.arange(_QK_TN)[None, :]
    quarter = HEAD_DIM // 4
    odd = (dst // quarter) % 2
    rot = jnp.where((odd == 0) & (src == dst + quarter), -1.0, 0.0) + jnp.where((odd == 1) & (src == dst - quarter), 1.0, 0.0)
    mean = jnp.where(src // HEAD_DIM == dst // HEAD_DIM, 1.0 / HEAD_DIM, 0.0)
    twice = lambda m: jnp.concatenate([m, m], axis=0).astype(BF16)
    return cos, sin, twice(mean), twice(rot)


def _even_proj(n, w_in, q_g, k_g, rope, i):
    tn = _QK_TN
    gains = jnp.stack([q_g, k_g]).reshape(2, 1, HEAD_DIM)
    tab_spec = pl.BlockSpec((S, tn), lambda b, j: (0, 0))
    mat_spec = pl.BlockSpec((2 * tn, tn), lambda b, j: (0, 0))
    qk = _proj_call(_qk_kernel, n, w_in, i, 0, A_Q_W + A_KV_W, tn, (gains,) + rope,
                    [pl.BlockSpec((1, 1, HEAD_DIM), lambda b, j: (jnp.where(j < A_Q_W // tn, 0, 1), 0, 0)),
                     tab_spec, tab_spec, mat_spec, mat_spec], BF16, "even_qk")
    v = _proj_call(_plain_kernel, n, w_in, i, (A_Q_W + A_KV_W) // tn, A_KV_W, tn, (), [], BF16, "even_v")
    glu0 = (A_Q_W + 2 * A_KV_W) // tn
    glu = _proj_call(_glu_kernel, n, w_in, i, glu0, CONV_CH, tn, (), [], BF16, "even_glu",
                     w_blk_offsets=(0, CONV_CH // tn))
    return qk, v, glu


def _mm_res_kernel(*refs, n_x, k_sizes, tm):
    x_refs = refs[:n_x]
    w_ref, h_ref, g_ref, o_ref = refs[n_x:]
    i = pl.program_id(0)
    acc = None
    off = 0
    for x_ref, kx in zip(x_refs, k_sizes):
        part = _dot(x_ref[...], w_ref[0, off:off + kx, :].astype(BF16))
        acc = part if acc is None else acc + part
        off += kx
    tiles_per_sample = S // tm
    b = i // tiles_per_sample
    row = lax.broadcasted_iota(jnp.int32, (tm, 1), 0) + (i % tiles_per_sample) * tm
    gate = jnp.where(row < CTX, g_ref[NB:NB + 1, :], g_ref[pl.ds(b, 1), :])
    o_ref[...] = h_ref[...] + gate * acc


def _mm_res(xs, w, layer, h, gate, name):
    tn, tm = 256, S // 2
    k_sizes = tuple(x.shape[-1] for x in xs)
    k = sum(k_sizes)
    m = NB * S
    kern = functools.partial(_mm_res_kernel, n_x=len(xs), k_sizes=k_sizes, tm=tm)
    return pl.pallas_call(
        kern,
        grid=(m // tm, D // tn),
        in_specs=[pl.BlockSpec((tm, kx), lambda i, j: (i, 0)) for kx in k_sizes] + [
            pl.BlockSpec((1, k, tn), lambda i, j: (layer, 0, j)),
            pl.BlockSpec((tm, tn), lambda i, j: (i, j)),
            pl.BlockSpec((8, tn), lambda i, j: (0, j)),
        ],
        out_specs=pl.BlockSpec((tm, tn), lambda i, j: (i, j)),
        out_shape=jax.ShapeDtypeStruct((m, D), F32),
        compiler_params=_params(("parallel", "parallel")),
        name=name,
    )(*xs, w, h, gate)


_CPAD = 16
_CROWS = _CPAD + CTX + _CPAD + SEQ + _CPAD
_CBLK = 256


def _conformer_kernel(x_ref, w_ref, b_ref, g_ref, beta_ref, o_ref, pad_s):
    zeros = jnp.zeros((_CPAD, 128), F32)
    pad_s[0:_CPAD, :] = zeros
    pad_s[_CPAD + CTX:2 * _CPAD + CTX, :] = zeros
    pad_s[_CROWS - _CPAD:_CROWS, :] = zeros
    pad_s[_CPAD:_CPAD + CTX, :] = x_ref[0, 0:CTX, :].astype(F32)
    pad_s[2 * _CPAD + CTX:2 * _CPAD + S, :] = x_ref[0, CTX:S, :].astype(F32)
    w = w_ref[0]
    left = (CONV_K - 1) // 2
    for blk in range(S // _CBLK):
        r0 = blk * _CBLK
        p0 = r0 + (_CPAD if r0 < CTX else 2 * _CPAD)
        acc = jnp.zeros((_CBLK, 128), F32)
        for k in range(CONV_K):
            acc = acc + w[k:k + 1, :] * pad_s[p0 + k - left:p0 + k - left + _CBLK, :]
        hcv = acc + b_ref[0]
        mu = jnp.mean(hcv, axis=-1, keepdims=True)
        dlt = hcv - mu
        var = jnp.mean(dlt * dlt, axis=-1, keepdims=True)
        hn = dlt * lax.rsqrt(var + EPS) * g_ref[0] + beta_ref[0]
        o_ref[0, r0:r0 + _CBLK, :] = _silu(hn).astype(BF16)


def _conformer(glu, dw_w, dw_b, ln_g, ln_b, i):
    groups = CONV_CH // 128
    vec = lambda a: a.reshape(a.shape[0], 1, CONV_CH)
    vspec = pl.BlockSpec((1, 1, 128), lambda b, c: (i, 0, c))
    return pl.pallas_call(
        _conformer_kernel,
        grid=(NB, groups),
        in_specs=[
            pl.BlockSpec((1, S, 128), lambda b, c: (b, 0, c)),
            pl.BlockSpec((1, CONV_K, 128), lambda b, c: (i, 0, c)),
            vspec, vspec, vspec,
        ],
        out_specs=pl.BlockSpec((1, S, 128), lambda b, c: (b, 0, c)),
        out_shape=jax.ShapeDtypeStruct((NB, S, CONV_CH), BF16),
        scratch_shapes=[pltpu.VMEM((_CROWS, 128), F32)],
        compiler_params=_params(("parallel", "parallel")),
        name="conformer_conv",
    )(glu, dw_w, vec(dw_b), vec(ln_g), vec(ln_b))


def _attn_kernel(sink_ref, q_ref, kc_ref, vc_ref, *rest, band, layer_row):
    if band:
        kp_ref, kn_ref, kx_ref, vp_ref, vn_ref, vx_ref, mask_ref, o_ref = rest
        kv_heads = [(hv, hv) for hv in range(A_KV_HEADS)]
    else:
        (o_ref,) = rest
        kv_heads = [(pl.program_id(1), 0)]
    nq = q_ref.shape[1]
    scale = HEAD_DIM ** -0.5
    rows = A_GROUP * nq
    gidx = lax.broadcasted_iota(jnp.int32, (rows, 1), 0) // nq
    for h, cblk in kv_heads:
        hd = slice(cblk * HEAD_DIM, (cblk + 1) * HEAD_DIM)
        q0 = cblk * A_GROUP * HEAD_DIM
        q = jnp.concatenate([q_ref[0, :, q0 + g * HEAD_DIM:q0 + (g + 1) * HEAD_DIM] for g in range(A_GROUP)], axis=0)
        if band:
            kcat = jnp.concatenate([kc_ref[0, :, hd], kp_ref[0, :, hd], kn_ref[0, :, hd], kx_ref[0, :, hd]], axis=0)
            vcat = jnp.concatenate([vc_ref[0, :, hd], vp_ref[0, :, hd], vn_ref[0, :, hd], vx_ref[0, :, hd]], axis=0)
        else:
            kcat, vcat = kc_ref[0], vc_ref[0]
        s = _dot_nt(q, kcat) * scale
        if band:
            s = s + mask_ref[0]
        sink = jnp.zeros((rows, 1), F32)
        for g in range(A_GROUP):
            sink = jnp.where(gidx == g, sink_ref[layer_row, h * A_GROUP + g], sink)
        m = jnp.maximum(jnp.max(s, axis=-1, keepdims=True), sink)
        p = jnp.exp(s - m)
        den = jnp.sum(p, axis=-1, keepdims=True) + jnp.exp(sink - m)
        o = _dot(p.astype(BF16), vcat) / den
        for g in range(A_GROUP):
            o_ref[0, :, q0 + g * HEAD_DIM:q0 + (g + 1) * HEAD_DIM] = o[g * nq:(g + 1) * nq].astype(BF16)


def _attention(qk, v, sink, i):
    gw = A_GROUP * HEAD_DIM
    kcol0 = A_Q_W // HEAD_DIM
    cb = CTX // ABLK
    nb = SEQ // ABLK
    smem = pl.BlockSpec(memory_space=pltpu.SMEM)

    def blk(rows, col_blk, row_fn):
        return pl.BlockSpec((1, rows, A_KV_W), lambda b, n: (b, row_fn(n), col_blk))

    kcol = A_Q_W // A_KV_W
    vcol = 0
    prev_r = lambda n: cb + jnp.maximum(n - 1, 0)
    cur_r = lambda n: cb + n
    next_r = lambda n: cb + jnp.minimum(n + 1, nb - 1)
    zero_r = lambda n: 0
    r = jnp.arange(A_GROUP * ABLK)[:, None] % ABLK
    c = jnp.arange(CTX + 3 * ABLK)[None, :] - CTX
    in_window = jnp.abs(c - ABLK - r) <= WINDOW
    ok = [(c < 0) | (in_window & keep) for keep in (c >= ABLK, c >= 0, c < 2 * ABLK)]
    masks = jnp.where(jnp.stack(ok), 0.0, -jnp.inf).astype(F32)
    mask_spec = pl.BlockSpec((1, A_GROUP * ABLK, CTX + 3 * ABLK),
                             lambda b, n: (jnp.where(n == 0, 0, jnp.where(n == nb - 1, 2, 1)), 0, 0))
    lat = pl.pallas_call(
        functools.partial(_attn_kernel, band=True, layer_row=i),
        grid=(NB, nb),
        in_specs=[
            smem,
            pl.BlockSpec((1, ABLK, A_Q_W), lambda b, n: (b, cb + n, 0)),
            blk(CTX, kcol, zero_r), blk(CTX, vcol, zero_r),
            blk(ABLK, kcol, prev_r), blk(ABLK, kcol, cur_r), blk(ABLK, kcol, next_r),
            blk(ABLK, vcol, prev_r), blk(ABLK, vcol, cur_r), blk(ABLK, vcol, next_r),
            mask_spec,
        ],
        out_specs=pl.BlockSpec((1, ABLK, A_Q_W), lambda b, n: (b, n, 0)),
        out_shape=jax.ShapeDtypeStruct((NB, SEQ, A_Q_W), BF16),
        compiler_params=_params(("parallel", "parallel")),
        name="attn_latent",
    )(sink, qk, qk, v, qk, qk, qk, v, v, v, masks)
    ctx = pl.pallas_call(
        functools.partial(_attn_kernel, band=False, layer_row=i),
        grid=(NB, A_KV_HEADS),
        in_specs=[
            smem,
            pl.BlockSpec((1, CTX, gw), lambda b, h: (b, 0, h)),
            pl.BlockSpec((1, CTX, HEAD_DIM), lambda b, h: (b, 0, kcol0 + h)),
            pl.BlockSpec((1, CTX, HEAD_DIM), lambda b, h: (b, 0, h)),
        ],
        out_specs=pl.BlockSpec((1, CTX, gw), lambda b, h: (b, 0, h)),
        out_shape=jax.ShapeDtypeStruct((NB, CTX, A_Q_W), BF16),
        compiler_params=_params(("parallel", "parallel")),
        name="attn_context",
    )(sink, qk, qk, v)
    return jnp.concatenate([ctx, lat], axis=1)


def _dn_conv_kernel(x_ref, w_ref, cw_ref, o_ref, *, tn, sub, l2norm):
    j = pl.program_id(1)
    x = x_ref[0]
    for s in range(sub):
        y = _dot(x, w_ref[0, :, s * tn:(s + 1) * tn].astype(BF16))
        a = _silu(_seg_conv(y, cw_ref[0, :, s * tn:(s + 1) * tn]))
        if not l2norm:
            o_ref[0, :, s * tn:(s + 1) * tn] = a.astype(BF16)
            continue
        qscale = jnp.where((j * sub + s) * tn < DN_QK_W, DN_DK ** -0.5, 1.0)
        for hh in range(tn // DN_DK):
            c0 = s * tn + hh * DN_DK
            t = a[:, hh * DN_DK:(hh + 1) * DN_DK]
            inv = lax.rsqrt(jnp.sum(t * t, axis=-1, keepdims=True) + EPS) * qscale
            o_ref[0, :, c0:c0 + DN_DK] = (t * inv).astype(BF16)


_GATE_ROWS = 4 * CHUNK


def _dn_gate_kernel(x_ref, w_ref, alog_ref, dtb_ref, o_ref, w_s):
    @pl.when((pl.program_id(0) == 0) & (pl.program_id(1) == 0))
    def _():
        w_s[...] = w_ref[0].astype(BF16)

    ba = _dot(x_ref[0], w_s[...])
    beta = jax.nn.sigmoid(ba)
    g = -jnp.exp(alog_ref[...]) * jax.nn.softplus(ba + dtb_ref[...])
    ii = lax.broadcasted_iota(jnp.int32, (_GATE_ROWS, _GATE_ROWS), 0)
    jj = lax.broadcasted_iota(jnp.int32, (_GATE_ROWS, _GATE_ROWS), 1)
    same_chunk = (ii // CHUNK) == (jj // CHUNK)
    tri_f = jnp.where(same_chunk & (ii >= jj), 1.0, 0.0).astype(BF16)
    tri_b = jnp.where(same_chunk & (ii <= jj), 1.0, 0.0).astype(BF16)
    g1 = g.astype(BF16)
    r1 = g - g1.astype(F32)
    g2 = r1.astype(BF16)
    g3 = (r1 - g2.astype(F32)).astype(BF16)
    cs_f = _dot(tri_f, g1) + _dot(tri_f, g2) + _dot(tri_f, g3)
    cs_b = _dot(tri_b, g1) + _dot(tri_b, g2) + _dot(tri_b, g3)
    col = lax.broadcasted_iota(jnp.int32, (_GATE_ROWS, 128), 1)
    o_ref[0] = jnp.where(col < 2 * DN_V_HEADS, beta, jnp.where(col < 3 * DN_V_HEADS, cs_f, cs_b))


def _dn_project(n, w_in, conv_w, a_log, dt_bias, i):
    tn, sub = 256, 2
    tb = tn * sub

    def conv_proj(col0, width, l2norm, name):
        blk0 = col0 // tb
        return _proj_call(functools.partial(_dn_conv_kernel, tn=tn, sub=sub, l2norm=l2norm), n, w_in, i, blk0,
                          width, tb, (conv_w,),
                          [pl.BlockSpec((1, DN_SHORT_K, tb), lambda b, j: (i, 0, blk0 + j))], BF16, name)

    qk = conv_proj(0, 2 * DN_QK_W, True, "dn_qk")
    vv = conv_proj(2 * DN_QK_W, DN_V_W, False, "dn_v")
    z = _proj_call(_plain_kernel, n, w_in, i, DN_CONV_CH // tn, DN_V_W, tn, (), [], BF16, "dn_z")
    zeros = jnp.zeros((2 * DN_V_HEADS,), F32)
    alog = jnp.concatenate([zeros, a_log[i].reshape(-1)]).reshape(1, 128)
    dtb = jnp.concatenate([zeros, dt_bias[i].reshape(-1)]).reshape(1, 128)
    gates = pl.pallas_call(
        _dn_gate_kernel,
        grid=(NB, S // _GATE_ROWS),
        in_specs=[
            pl.BlockSpec((1, _GATE_ROWS, D), lambda b, c: (b, c, 0)),
            pl.BlockSpec((1, D, 128), lambda b, c: (i, 0, (DN_CONV_CH + DN_V_W) // 128)),
            pl.BlockSpec((1, 128), lambda b, c: (0, 0)),
            pl.BlockSpec((1, 128), lambda b, c: (0, 0)),
        ],
        out_specs=pl.BlockSpec((1, _GATE_ROWS, 128), lambda b, c: (b, c, 0)),
        out_shape=jax.ShapeDtypeStruct((NB, S, 128), F32),
        scratch_shapes=[pltpu.VMEM((D, 128), BF16)],
        compiler_params=_params(("arbitrary", "arbitrary")),
        name="dn_gates",
    )(n, w_in, alog, dtb)
    return qk, vv, z, gates


_DN_G = 32
_DN_HG = DN_V_HEADS // _DN_G
_DN_REP = DN_V_HEADS // DN_K_HEADS


def _bmm(a, b):
    return lax.dot_general(a, b, (((2,), (1,)), ((0,), (0,))), preferred_element_type=F32)


def _bmm_nt(a, b):
    return lax.dot_general(a, b, (((2,), (2,)), ((0,), (0,))), preferred_element_type=F32)


def _bmm_tn(a, b):
    return lax.dot_general(a, b, (((1,), (1,)), ((0,), (0,))), preferred_element_type=F32)


def _hi_lo(x):
    hi = x.astype(BF16).astype(F32)
    return hi, x - hi


def _bmm_f32_lhs(a, b):
    ah, al = _hi_lo(a)
    lhs = jnp.concatenate([ah, al], axis=2).astype(BF16)
    return _bmm(lhs, jnp.concatenate([b, b], axis=1))


def _bmm_f32(a, b):
    ah, al = _hi_lo(a)
    bh, bl = _hi_lo(b)
    lhs = jnp.concatenate([ah, al, ah], axis=2).astype(BF16)
    rhs = jnp.concatenate([bh, bh, bl], axis=1).astype(BF16)
    return _bmm(lhs, rhs)


def _dn_scan_kernel(q_ref, k_ref, v_ref, g1_ref, g2_ref, o_ref, s_ref):
    d = pl.program_id(1)
    step = pl.program_id(3)
    ng, nk = _DN_G, _DN_G // _DN_REP

    @pl.when(step == 0)
    def _():
        s_ref[...] = jnp.zeros_like(s_ref)

    fwd = d == 0
    ii = lax.broadcasted_iota(jnp.int32, (CHUNK, CHUNK), 0)
    jj = lax.broadcasted_iota(jnp.int32, (CHUNK, CHUNK), 1)
    later = (ii - jj) * jnp.where(fwd, 1, -1)
    incl = later >= 0
    strict = jnp.where(later > 0, 1.0, 0.0)
    eye = jnp.where(ii == jj, 1.0, 0.0)

    def same_block(size):
        return jnp.where((ii // size) == (jj // size), 1.0, 0.0)

    g1 = g1_ref[0, 0, 0]
    g2 = g2_ref[0, 0, 0, 0]
    q3 = jnp.stack([q_ref[0, :, i * DN_DK:(i + 1) * DN_DK] for i in range(nk)])
    k3 = jnp.stack([k_ref[0, :, i * DN_DK:(i + 1) * DN_DK] for i in range(nk)])
    v3 = jnp.stack([v_ref[0, :, e * DN_DV:(e + 1) * DN_DV] for e in range(ng)])
    beta_r = jnp.stack([g2[e:e + 1, :] for e in range(ng)])
    gc_r = jnp.stack([g2[ng + e:ng + e + 1, :] for e in range(ng)])
    beta_c = jnp.stack([jnp.broadcast_to(g1[:, e:e + 1], (CHUNK, CHUNK)) for e in range(ng)])
    gc_c = jnp.stack([jnp.broadcast_to(g1[:, ng + e:ng + e + 1], (CHUNK, DN_DK)) for e in range(ng)])
    g_last = jnp.where(fwd, gc_r[:, :, CHUNK - 1:CHUNK], gc_r[:, :, 0:1])

    decay = jnp.exp(jnp.where(incl, gc_c[:, :, :CHUNK] - gc_r, -jnp.inf))
    kk = jnp.repeat(_bmm_nt(k3, k3), _DN_REP, axis=0)
    qk = jnp.repeat(_bmm_nt(q3, k3), _DN_REP, axis=0)
    a = kk * (beta_c * decay) * strict

    def mm(x, y):
        return _bmm(x.astype(BF16), y.astype(BF16))

    d4 = a * same_block(4)
    imd = eye - d4
    t = imd + mm(imd, mm(d4, d4))
    size = 4
    while size < CHUNK:
        n_s = a * (same_block(2 * size) - same_block(size))
        t = t - mm(t, mm(n_s, t))
        size *= 2
    t = t + mm(t, eye - t - _bmm_f32(a, t))

    k_v = jnp.repeat(k3, _DN_REP, axis=0)
    tb = t * beta_r
    u = _bmm_f32_lhs(tb, v3)
    w = _bmm_f32_lhs(tb * jnp.exp(gc_r), k_v)

    st = s_ref[...]
    stb = st.astype(BF16)
    qg = jnp.repeat(q3, _DN_REP, axis=0) * jnp.exp(gc_c).astype(BF16)
    ws = _bmm(jnp.concatenate([w.astype(BF16), qg], axis=1), stb)
    v_new = u - ws[:, :CHUNK]
    vnb = v_new.astype(BF16)
    o = ws[:, CHUNK:] + _bmm((qk * decay).astype(BF16), vnb)
    for e in range(ng):
        o_ref[0, 0, :, e * DN_DV:(e + 1) * DN_DV] = o[e].astype(o_ref.dtype)
    kd = k_v * jnp.exp(g_last - gc_c).astype(BF16)
    s_ref[...] = st * jnp.exp(g_last) + _bmm_tn(kd, vnb)


def _dn_scan(qk, vv, gates):
    hpg = DN_V_HEADS // _DN_HG
    kw = (DN_K_HEADS // _DN_HG) * DN_DK
    vw = hpg * DN_DV
    bt = gates[..., :2 * DN_V_HEADS].reshape(NB, S, 2, _DN_HG, hpg)
    gc = gates[..., 2 * DN_V_HEADS:].reshape(NB, S, 2, _DN_HG, hpg)
    g1 = jnp.transpose(jnp.concatenate([bt, gc], axis=-1), (0, 2, 3, 1, 4))
    g2 = jnp.swapaxes(g1.reshape(NB, 2, _DN_HG, NCH, CHUNK, 2 * hpg), -1, -2)

    def chunk(d, s):
        return jnp.where(d == 0, s, jnp.where(s < CTX_CH, CTX_CH - 1 - s, NCH + CTX_CH - 1 - s))

    return pl.pallas_call(
        _dn_scan_kernel,
        grid=(NB, 2, _DN_HG, NCH),
        in_specs=[
            pl.BlockSpec((1, CHUNK, kw), lambda b, d, g, s: (b, chunk(d, s), g)),
            pl.BlockSpec((1, CHUNK, kw), lambda b, d, g, s: (b, chunk(d, s), DN_QK_W // kw + g)),
            pl.BlockSpec((1, CHUNK, vw), lambda b, d, g, s: (b, chunk(d, s), g)),
            pl.BlockSpec((1, 1, 1, CHUNK, 2 * hpg), lambda b, d, g, s: (b, d, g, chunk(d, s), 0)),
            pl.BlockSpec((1, 1, 1, 1, 2 * hpg, CHUNK), lambda b, d, g, s: (b, d, g, chunk(d, s), 0, 0)),
        ],
        out_specs=pl.BlockSpec((1, 1, CHUNK, vw), lambda b, d, g, s: (b, d, chunk(d, s), g)),
        out_shape=jax.ShapeDtypeStruct((NB, 2, S, DN_V_W), BF16),
        scratch_shapes=[pltpu.VMEM((hpg, DN_DK, DN_DV), F32)],
        compiler_params=_params(("parallel", "parallel", "parallel", "arbitrary")),
        name="dn_scan",
    )(qk, qk, vv, g1, g2)


def _dn_out_kernel(o_ref, z_ref, g_ref, y_ref):
    g = g_ref[...]
    for hh in range(o_ref.shape[-1] // DN_DV):
        sl = slice(hh * DN_DV, (hh + 1) * DN_DV)
        o = o_ref[0, 0, :, sl].astype(F32) + o_ref[0, 1, :, sl].astype(F32)
        o = o * lax.rsqrt(jnp.mean(o * o, axis=-1, keepdims=True) + EPS) * g
        y_ref[0, :, sl] = (o * _silu(z_ref[0, :, sl].astype(F32))).astype(BF16)


def _dn_gated_norm(o, z, norm_g):
    ts, tw = S // 3, 1024
    return pl.pallas_call(
        _dn_out_kernel,
        grid=(NB, S // ts, DN_V_W // tw),
        in_specs=[
            pl.BlockSpec((1, 2, ts, tw), lambda b, t, c: (b, 0, t, c)),
            pl.BlockSpec((1, ts, tw), lambda b, t, c: (b, t, c)),
            pl.BlockSpec((1, DN_DV), lambda b, t, c: (0, 0)),
        ],
        out_specs=pl.BlockSpec((1, ts, tw), lambda b, t, c: (b, t, c)),
        out_shape=jax.ShapeDtypeStruct((NB, S, DN_V_W), BF16),
        compiler_params=_params(("parallel", "parallel", "parallel")),
        name="dn_gated_norm",
    )(o, z, norm_g.reshape(1, DN_DV))


def kernel(x, c, ctx, c_ctx, w_mod, b_mod, norm1_g, norm2_g, ffn_w_up, ffn_conv_w, ffn_conv_b, ffn_w_down, even_w_in, even_w_out, attn_q_norm_g, attn_k_norm_g, attn_sink, conv_dw_w, conv_dw_b, conv_ln_g, conv_ln_b, dn_w_in, dn_conv_w, dn_a_log, dn_dt_bias, dn_norm_g, dn_w_out):
    cvec = jnp.concatenate([c, c_ctx[None, :], jnp.zeros((8 - NB - 1, D), F32)], axis=0)
    mods = _modulation(cvec, w_mod, b_mod).reshape(DEPTH, 8, N_MOD, D)
    rope = _rope_tables()
    h = jnp.concatenate([ctx, x], axis=1)
    m = NB * S
    for layer in range(DEPTH):
        i = layer // 2
        sh1, sc1, g1, sh2, sc2, g2 = [mods[layer, :, j, :] for j in range(N_MOD)]
        n = _norm_mod(h, norm1_g[layer], sc1.reshape(8, 1, D), sh1.reshape(8, 1, D))
        hf = h.reshape(m, D)
        if layer % 2 == 0:
            qk, v, glu = _even_proj(n, even_w_in, attn_q_norm_g[i], attn_k_norm_g[i], rope, i)
            att = _attention(qk, v, attn_sink, i)
            cv = _conformer(glu, conv_dw_w, conv_dw_b, conv_ln_g, conv_ln_b, i)
            hf = _mm_res([att.reshape(m, A_Q_W), cv.reshape(m, CONV_CH)], even_w_out, i, hf, g1, "even_out")
        else:
            qk, vv, z, gates = _dn_project(n, dn_w_in, dn_conv_w, dn_a_log, dn_dt_bias, i)
            o = _dn_scan(qk, vv, gates)
            y = _dn_gated_norm(o, z, dn_norm_g[i])
            hf = _mm_res([y.reshape(m, DN_V_W)], dn_w_out, i, hf, g1, "dn_out")
        h = hf.reshape(NB, S, D)
        n2 = _norm_mod(h, norm2_g[layer], sc2.reshape(8, 1, D), sh2.reshape(8, 1, D))
        f = _ffn_up(n2, ffn_w_up, ffn_conv_w, ffn_conv_b, layer)
        h = _mm_res([f.reshape(m, D_FF)], ffn_w_down, layer, h.reshape(m, D), g2, "ffn_down").reshape(NB, S, D)
    return h[:, CTX:, :]
```

```python
import functools
import math

import jax
import jax.numpy as jnp
from jax import lax
from jax.experimental import pallas as pl
from jax.experimental.pallas import tpu as pltpu

F32 = jnp.float32
BF16 = jnp.bfloat16

D = 2048
NB = 4
SEQ = 2048
CTX = 256
S = CTX + SEQ
DEPTH = 4
GRID_W = 64
EPS = 1e-6
N_MOD = 6

HEAD_DIM = 128
A_Q_HEADS = 8
A_KV_HEADS = 2
A_GROUP = 4
A_Q_W = 1024
A_KV_W = 256
WINDOW = 128
ABLK = 128
ROPE_THETA = 10000.0
CONV_CH = 1024
CONV_K = 31
EVEN_IN_W = 3584

DN_DK = 128
DN_DV = 128
DN_K_HEADS = 16
DN_V_HEADS = 32
DN_QK_W = 2048
DN_V_W = 4096
DN_CONV_CH = 8192
DN_BA_W = 128
DN_SHORT_K = 5
CHUNK = 64
NCH = S // CHUNK
CTX_CH = CTX // CHUNK

D_FF = 5632
FFN_CONV_K = 3

VMEM_LIMIT = 58 * 1024 * 1024


def _params(sem, vmem=VMEM_LIMIT):
    return pltpu.CompilerParams(dimension_semantics=sem, vmem_limit_bytes=vmem)


def _silu(x):
    return x * jax.nn.sigmoid(x)


def _dot(a, b):
    return jnp.dot(a, b, preferred_element_type=F32)


def _dot_nt(a, b):
    return lax.dot_general(a, b, (((1,), (1,)), ((), ())), preferred_element_type=F32)


def _dot_tn(a, b):
    return lax.dot_general(a, b, (((0,), (0,)), ((), ())), preferred_element_type=F32)


_SUB = 8


def _seg_conv(y, cw):
    taps, n = cw.shape
    left = (taps - 1) // 2
    assert left < _SUB
    y3 = y.reshape(S // _SUB, _SUB, n)
    sub = lax.broadcasted_iota(jnp.int32, (1, _SUB, n), 1)
    nc = CTX // _SUB
    zero = jnp.zeros((1, _SUB, n), F32)
    acc = cw[left:left + 1] * y3
    for k in range(taps):
        s = k - left
        if s == 0:
            continue
        r = pltpu.roll(y3, (-s) % _SUB, 1)
        if s > 0:
            other = jnp.concatenate([r[1:nc], zero, r[nc + 1:], zero], axis=0)
            shifted = jnp.where(sub < _SUB - s, r, other)
        else:
            other = jnp.concatenate([zero, r[:nc - 1], zero, r[nc:-1]], axis=0)
            shifted = jnp.where(sub >= -s, r, other)
        acc = acc + cw[k:k + 1] * shifted
    return acc.reshape(S, n)


def _mod_kernel(c_ref, w_ref, b_ref, o_ref):
    a = _silu(c_ref[...]).astype(BF16)
    o_ref[0] = _dot(a, w_ref[0].astype(BF16)) + b_ref[0]


def _modulation(cvec, w_mod, b_mod):
    tn = 1024
    n = N_MOD * D
    return pl.pallas_call(
        _mod_kernel,
        grid=(DEPTH, n // tn),
        in_specs=[
            pl.BlockSpec((8, D), lambda l, j: (0, 0)),
            pl.BlockSpec((1, D, tn), lambda l, j: (l, 0, j)),
            pl.BlockSpec((1, 1, tn), lambda l, j: (l, 0, j)),
        ],
        out_specs=pl.BlockSpec((1, 8, tn), lambda l, j: (l, 0, j)),
        out_shape=jax.ShapeDtypeStruct((DEPTH, 8, n), F32),
        compiler_params=_params(("parallel", "parallel")),
        name="adaln_mod",
    )(cvec, w_mod, b_mod.reshape(DEPTH, 1, n))


def _norm_mod_kernel(h_ref, g_ref, sc_ref, sh_ref, o_ref):
    x = h_ref[0]
    y = x * lax.rsqrt(jnp.mean(x * x, axis=-1, keepdims=True) + EPS) * g_ref[...]
    o_ref[0] = (y * (1.0 + sc_ref[0]) + sh_ref[0]).astype(BF16)


def _norm_mod(h, g, sc, sh):
    ts = CTX
    mod_spec = pl.BlockSpec((1, 1, D), lambda b, t: (jnp.where(t == 0, NB, b), 0, 0))
    return pl.pallas_call(
        _norm_mod_kernel,
        grid=(NB, S // ts),
        in_specs=[
            pl.BlockSpec((1, ts, D), lambda b, t: (b, t, 0)),
            pl.BlockSpec((1, D), lambda b, t: (0, 0)),
            mod_spec,
            mod_spec,
        ],
        out_specs=pl.BlockSpec((1, ts, D), lambda b, t: (b, t, 0)),
        out_shape=jax.ShapeDtypeStruct((NB, S, D), BF16),
        compiler_params=_params(("parallel", "parallel")),
        name="norm_mod",
    )(h, g.reshape(1, D), sc, sh)


def _xw(x_ref, w_ref):
    return _dot(x_ref[0], w_ref[0].astype(BF16))


def _proj_call(kernel, x, w, layer, col_blk0, n_out, tn, extra, extra_specs, out_dtype, name,
               w_blk_offsets=(0,), scratch=(), x_single=False):
    k = x.shape[-1]
    w_specs = [pl.BlockSpec((1, k, tn), functools.partial(lambda b, j, off: (layer, 0, col_blk0 + off + j), off=off))
               for off in w_blk_offsets]
    x_mode = {"pipeline_mode": pl.Buffered(1)} if x_single else {}
    return pl.pallas_call(
        kernel,
        grid=(NB, n_out // tn),
        in_specs=[pl.BlockSpec((1, S, k), lambda b, j: (b, 0, 0), **x_mode)] + w_specs + extra_specs,
        out_specs=pl.BlockSpec((1, S, tn), lambda b, j: (b, 0, j)),
        out_shape=jax.ShapeDtypeStruct((NB, S, n_out), out_dtype),
        scratch_shapes=list(scratch),
        compiler_params=_params(("parallel", "parallel")),
        name=name,
    )(x, *([w] * len(w_blk_offsets)), *extra)


def _ffn_up_kernel(x_ref, wg_ref, wv_ref, cw_ref, cb_ref, o_ref, *, tn, sub):
    x = x_ref[0]
    tiles = [slice(s * tn, (s + 1) * tn) for s in range(sub)]
    gates = [_dot(x, wg_ref[0, :, cols].astype(BF16)) for cols in tiles]
    vals = [_dot(x, wv_ref[0, :, cols].astype(BF16)) for cols in tiles]
    for cols, gate, val in zip(tiles, gates, vals):
        conv = _seg_conv(gate, cw_ref[0, :, cols]) + cb_ref[0, :, cols]
        o_ref[0, :, cols] = (_silu(conv) * val).astype(BF16)


def _ffn_up(n, w_up, conv_w, conv_b, layer):
    tn, sub = 256, 2
    tb = tn * sub
    nblk = D_FF // tb
    extra_specs = [
        pl.BlockSpec((1, FFN_CONV_K, tb), lambda b, j: (layer, 0, j)),
        pl.BlockSpec((1, 1, tb), lambda b, j: (layer, 0, j)),
    ]
    return _proj_call(functools.partial(_ffn_up_kernel, tn=tn, sub=sub), n, w_up, layer, 0, D_FF, tb,
                      (conv_w, conv_b.reshape(DEPTH, 1, D_FF)), extra_specs, BF16, "ffn_up",
                      w_blk_offsets=(0, nblk), x_single=True)


def _hi_lo_cols(z):
    hi = z.astype(BF16)
    return jnp.concatenate([hi, (z - hi.astype(F32)).astype(BF16)], axis=1)


def _qk_kernel(x_ref, w_ref, g_ref, cos_ref, sin_ref, mean_ref, rot_ref, o_ref):
    y = _xw(x_ref, w_ref)
    g = jnp.concatenate([g_ref[0]] * (y.shape[1] // HEAD_DIM), axis=1)
    ms = _dot(_hi_lo_cols(y * y), mean_ref[...])
    t = y * lax.rsqrt(ms + EPS) * g
    r = t * cos_ref[...] + _dot(_hi_lo_cols(t), rot_ref[...]) * sin_ref[...]
    o_ref[0] = r.astype(BF16)


def _plain_kernel(x_ref, w_ref, o_ref):
    o_ref[0] = _xw(x_ref, w_ref).astype(o_ref.dtype)


def _glu_kernel(x_ref, wa_ref, wb_ref, o_ref):
    a = _xw(x_ref, wa_ref)
    b = _xw(x_ref, wb_ref)
    o_ref[0] = (a * jax.nn.sigmoid(b)).astype(BF16)


_QK_TN = 256


def _rope_tables():
    row = jnp.repeat(jnp.arange(SEQ // GRID_W, dtype=F32), GRID_W)
    col = jnp.tile(jnp.arange(GRID_W, dtype=F32), SEQ // GRID_W)
    half = HEAD_DIM // 2
    inv_freq = ROPE_THETA ** (-jnp.arange(0, half, 2, dtype=F32) / half)
    ang_r = row[:, None] * inv_freq
    ang_c = col[:, None] * inv_freq
    ang = jnp.concatenate([ang_r, ang_r, ang_c, ang_c], axis=-1)
    reps = _QK_TN // HEAD_DIM
    cos = jnp.tile(jnp.concatenate([jnp.ones((CTX, HEAD_DIM), F32), jnp.cos(ang)]), (1, reps))
    sin = jnp.tile(jnp.concatenate([jnp.zeros((CTX, HEAD_DIM), F32), jnp.sin(ang)]), (1, reps))
    src = jnp.arange(_QK_TN)[:, None]
    dst = jnp.arange(_QK_TN)[None, :]
    quarter = HEAD_DIM // 4
    odd = (dst // quarter) % 2
    rot = jnp.where((odd == 0) & (src == dst + quarter), -1.0, 0.0) + jnp.where((odd == 1) & (src == dst - quarter), 1.0, 0.0)
    mean = jnp.where(src // HEAD_DIM == dst // HEAD_DIM, 1.0 / HEAD_DIM, 0.0)
    twice = lambda m: jnp.concatenate([m, m], axis=0).astype(BF16)
    return cos, sin, twice(mean), twice(rot)


def _even_proj(n, w_in, q_g, k_g, rope, i):
    tn = _QK_TN
    gains = jnp.stack([q_g, k_g]).reshape(2, 1, HEAD_DIM)
    tab_spec = pl.BlockSpec((S, tn), lambda b, j: (0, 0))
    mat_spec = pl.BlockSpec((2 * tn, tn), lambda b, j: (0, 0))
    qk = _proj_call(_qk_kernel, n, w_in, i, 0, A_Q_W + A_KV_W, tn, (gains,) + rope,
                    [pl.BlockSpec((1, 1, HEAD_DIM), lambda b, j: (jnp.where(j < A_Q_W // tn, 0, 1), 0, 0)),
                     tab_spec, tab_spec, mat_spec, mat_spec], BF16, "even_qk")
    v = _proj_call(_plain_kernel, n, w_in, i, (A_Q_W + A_KV_W) // tn, A_KV_W, tn, (), [], BF16, "even_v")
    glu0 = (A_Q_W + 2 * A_KV_W) // tn
    glu = _proj_call(_glu_kernel, n, w_in, i, glu0, CONV_CH, tn, (), [], BF16, "even_glu",
                     w_blk_offsets=(0, CONV_CH // tn))
    return qk, v, glu


def _mm_res_kernel(*refs, n_x, k_sizes, tm):
    x_refs = refs[:n_x]
    w_ref, h_ref, g_ref, o_ref = refs[n_x:]
    i = pl.program_id(0)
    acc = None
    off = 0
    for x_ref, kx in zip(x_refs, k_sizes):
        part = _dot(x_ref[...], w_ref[0, off:off + kx, :].astype(BF16))
        acc = part if acc is None else acc + part
        off += kx
    tiles_per_sample = S // tm
    b = i // tiles_per_sample
    row = lax.broadcasted_iota(jnp.int32, (tm, 1), 0) + (i % tiles_per_sample) * tm
    gate = jnp.where(row < CTX, g_ref[NB:NB + 1, :], g_ref[pl.ds(b, 1), :])
    o_ref[...] = h_ref[...] + gate * acc


def _mm_res(xs, w, layer, h, gate, name):
    tn, tm = 256, S // 2
    k_sizes = tuple(x.shape[-1] for x in xs)
    k = sum(k_sizes)
    m = NB * S
    kern = functools.partial(_mm_res_kernel, n_x=len(xs), k_sizes=k_sizes, tm=tm)
    return pl.pallas_call(
        kern,
        grid=(m // tm, D // tn),
        in_specs=[pl.BlockSpec((tm, kx), lambda i, j: (i, 0)) for kx in k_sizes] + [
            pl.BlockSpec((1, k, tn), lambda i, j: (layer, 0, j)),
            pl.BlockSpec((tm, tn), lambda i, j: (i, j)),
            pl.BlockSpec((8, tn), lambda i, j: (0, j)),
        ],
        out_specs=pl.BlockSpec((tm, tn), lambda i, j: (i, j)),
        out_shape=jax.ShapeDtypeStruct((m, D), F32),
        compiler_params=_params(("parallel", "parallel")),
        name=name,
    )(*xs, w, h, gate)


_CPAD = 16
_CROWS = _CPAD + CTX + _CPAD + SEQ + _CPAD
_CBLK = 256


def _conformer_kernel(x_ref, w_ref, b_ref, g_ref, beta_ref, o_ref, pad_s):
    zeros = jnp.zeros((_CPAD, 128), F32)
    pad_s[0:_CPAD, :] = zeros
    pad_s[_CPAD + CTX:2 * _CPAD + CTX, :] = zeros
    pad_s[_CROWS - _CPAD:_CROWS, :] = zeros
    pad_s[_CPAD:_CPAD + CTX, :] = x_ref[0, 0:CTX, :].astype(F32)
    pad_s[2 * _CPAD + CTX:2 * _CPAD + S, :] = x_ref[0, CTX:S, :].astype(F32)
    w = w_ref[0]
    left = (CONV_K - 1) // 2
    for blk in range(S // _CBLK):
        r0 = blk * _CBLK
        p0 = r0 + (_CPAD if r0 < CTX else 2 * _CPAD)
        acc = jnp.zeros((_CBLK, 128), F32)
        for k in range(CONV_K):
            acc = acc + w[k:k + 1, :] * pad_s[p0 + k - left:p0 + k - left + _CBLK, :]
        hcv = acc + b_ref[0]
        mu = jnp.mean(hcv, axis=-1, keepdims=True)
        dlt = hcv - mu
        var = jnp.mean(dlt * dlt, axis=-1, keepdims=True)
        hn = dlt * lax.rsqrt(var + EPS) * g_ref[0] + beta_ref[0]
        o_ref[0, r0:r0 + _CBLK, :] = _silu(hn).astype(BF16)


def _conformer(glu, dw_w, dw_b, ln_g, ln_b, i):
    groups = CONV_CH // 128
    vec = lambda a: a.reshape(a.shape[0], 1, CONV_CH)
    vspec = pl.BlockSpec((1, 1, 128), lambda b, c: (i, 0, c))
    return pl.pallas_call(
        _conformer_kernel,
        grid=(NB, groups),
        in_specs=[
            pl.BlockSpec((1, S, 128), lambda b, c: (b, 0, c)),
            pl.BlockSpec((1, CONV_K, 128), lambda b, c: (i, 0, c)),
            vspec, vspec, vspec,
        ],
        out_specs=pl.BlockSpec((1, S, 128), lambda b, c: (b, 0, c)),
        out_shape=jax.ShapeDtypeStruct((NB, S, CONV_CH), BF16),
        scratch_shapes=[pltpu.VMEM((_CROWS, 128), F32)],
        compiler_params=_params(("parallel", "parallel")),
        name="conformer_conv",
    )(glu, dw_w, vec(dw_b), vec(ln_g), vec(ln_b))


def _attn_kernel(sink_ref, q_ref, kc_ref, vc_ref, *rest, band, layer_row):
    if band:
        kp_ref, kn_ref, kx_ref, vp_ref, vn_ref, vx_ref, mask_ref, o_ref = rest
        kv_heads = [(hv, hv) for hv in range(A_KV_HEADS)]
    else:
        (o_ref,) = rest
        kv_heads = [(pl.program_id(1), 0)]
    nq = q_ref.shape[1]
    scale = HEAD_DIM ** -0.5
    rows = A_GROUP * nq
    gidx = lax.broadcasted_iota(jnp.int32, (rows, 1), 0) // nq
    for h, cblk in kv_heads:
        hd = slice(cblk * HEAD_DIM, (cblk + 1) * HEAD_DIM)
        q0 = cblk * A_GROUP * HEAD_DIM
        q = jnp.concatenate([q_ref[0, :, q0 + g * HEAD_DIM:q0 + (g + 1) * HEAD_DIM] for g in range(A_GROUP)], axis=0)
        if band:
            kcat = jnp.concatenate([kc_ref[0, :, hd], kp_ref[0, :, hd], kn_ref[0, :, hd], kx_ref[0, :, hd]], axis=0)
            vcat = jnp.concatenate([vc_ref[0, :, hd], vp_ref[0, :, hd], vn_ref[0, :, hd], vx_ref[0, :, hd]], axis=0)
        else:
            kcat, vcat = kc_ref[0], vc_ref[0]
        s = _dot_nt(q, kcat) * scale
        if band:
            s = s + mask_ref[0]
        sink = jnp.zeros((rows, 1), F32)
        for g in range(A_GROUP):
            sink = jnp.where(gidx == g, sink_ref[layer_row, h * A_GROUP + g], sink)
        m = jnp.maximum(jnp.max(s, axis=-1, keepdims=True), sink)
        p = jnp.exp(s - m)
        den = jnp.sum(p, axis=-1, keepdims=True) + jnp.exp(sink - m)
        o = _dot(p.astype(BF16), vcat) / den
        for g in range(A_GROUP):
            o_ref[0, :, q0 + g * HEAD_DIM:q0 + (g + 1) * HEAD_DIM] = o[g * nq:(g + 1) * nq].astype(BF16)


def _attention(qk, v, sink, i):
    gw = A_GROUP * HEAD_DIM
    kcol0 = A_Q_W // HEAD_DIM
    cb = CTX // ABLK
    nb = SEQ // ABLK
    smem = pl.BlockSpec(memory_space=pltpu.SMEM)

    def blk(rows, col_blk, row_fn):
        return pl.BlockSpec((1, rows, A_KV_W), lambda b, n: (b, row_fn(n), col_blk))

    kcol = A_Q_W // A_KV_W
    vcol = 0
    prev_r = lambda n: cb + jnp.maximum(n - 1, 0)
    cur_r = lambda n: cb + n
    next_r = lambda n: cb + jnp.minimum(n + 1, nb - 1)
    zero_r = lambda n: 0
    r = jnp.arange(A_GROUP * ABLK)[:, None] % ABLK
    c = jnp.arange(CTX + 3 * ABLK)[None, :] - CTX
    in_window = jnp.abs(c - ABLK - r) <= WINDOW
    ok = [(c < 0) | (in_window & keep) for keep in (c >= ABLK, c >= 0, c < 2 * ABLK)]
    masks = jnp.where(jnp.stack(ok), 0.0, -jnp.inf).astype(F32)
    mask_spec = pl.BlockSpec((1, A_GROUP * ABLK, CTX + 3 * ABLK),
                             lambda b, n: (jnp.where(n == 0, 0, jnp.where(n == nb - 1, 2, 1)), 0, 0))
    lat = pl.pallas_call(
        functools.partial(_attn_kernel, band=True, layer_row=i),
        grid=(NB, nb),
        in_specs=[
            smem,
            pl.BlockSpec((1, ABLK, A_Q_W), lambda b, n: (b, cb + n, 0)),
            blk(CTX, kcol, zero_r), blk(CTX, vcol, zero_r),
            blk(ABLK, kcol, prev_r), blk(ABLK, kcol, cur_r), blk(ABLK, kcol, next_r),
            blk(ABLK, vcol, prev_r), blk(ABLK, vcol, cur_r), blk(ABLK, vcol, next_r),
            mask_spec,
        ],
        out_specs=pl.BlockSpec((1, ABLK, A_Q_W), lambda b, n: (b, n, 0)),
        out_shape=jax.ShapeDtypeStruct((NB, SEQ, A_Q_W), BF16),
        compiler_params=_params(("parallel", "parallel")),
        name="attn_latent",
    )(sink, qk, qk, v, qk, qk, qk, v, v, v, masks)
    ctx = pl.pallas_call(
        functools.partial(_attn_kernel, band=False, layer_row=i),
        grid=(NB, A_KV_HEADS),
        in_specs=[
            smem,
            pl.BlockSpec((1, CTX, gw), lambda b, h: (b, 0, h)),
            pl.BlockSpec((1, CTX, HEAD_DIM), lambda b, h: (b, 0, kcol0 + h)),
            pl.BlockSpec((1, CTX, HEAD_DIM), lambda b, h: (b, 0, h)),
        ],
        out_specs=pl.BlockSpec((1, CTX, gw), lambda b, h: (b, 0, h)),
        out_shape=jax.ShapeDtypeStruct((NB, CTX, A_Q_W), BF16),
        compiler_params=_params(("parallel", "parallel")),
        name="attn_context",
    )(sink, qk, qk, v)
    return jnp.concatenate([ctx, lat], axis=1)


def _dn_conv_kernel(x_ref, w_ref, cw_ref, o_ref, *, tn, sub, l2norm):
    j = pl.program_id(1)
    x = x_ref[0]
    for s in range(sub):
        y = _dot(x, w_ref[0, :, s * tn:(s + 1) * tn].astype(BF16))
        conv = _seg_conv(y, cw_ref[0, :, s * tn:(s + 1) * tn])
        if not l2norm:
            o_ref[0, :, s * tn:(s + 1) * tn] = _silu(conv.astype(BF16))
            continue
        a = _silu(conv)
        qscale = jnp.where((j * sub + s) * tn < DN_QK_W, DN_DK ** -0.5, 1.0)
        for hh in range(tn // DN_DK):
            c0 = s * tn + hh * DN_DK
            t = a[:, hh * DN_DK:(hh + 1) * DN_DK]
            inv = lax.rsqrt(jnp.sum(t * t, axis=-1, keepdims=True) + EPS) * qscale
            o_ref[0, :, c0:c0 + DN_DK] = (t * inv).astype(BF16)


_GATE_ROWS = 4 * CHUNK


def _dn_gate_kernel(x_ref, w_ref, alog_ref, dtb_ref, o_ref, w_s):
    @pl.when((pl.program_id(0) == 0) & (pl.program_id(1) == 0))
    def _():
        w_s[...] = w_ref[0].astype(BF16)

    ba = _dot(x_ref[0], w_s[...])
    beta = jax.nn.sigmoid(ba)
    g = -jnp.exp(alog_ref[...]) * jax.nn.softplus(ba + dtb_ref[...])
    ii = lax.broadcasted_iota(jnp.int32, (_GATE_ROWS, _GATE_ROWS), 0)
    jj = lax.broadcasted_iota(jnp.int32, (_GATE_ROWS, _GATE_ROWS), 1)
    same_chunk = (ii // CHUNK) == (jj // CHUNK)
    tri_f = jnp.where(same_chunk & (ii >= jj), 1.0, 0.0).astype(BF16)
    tri_b = jnp.where(same_chunk & (ii <= jj), 1.0, 0.0).astype(BF16)
    g1 = g.astype(BF16)
    r1 = g - g1.astype(F32)
    g2 = r1.astype(BF16)
    g3 = (r1 - g2.astype(F32)).astype(BF16)
    cs_f = _dot(tri_f, g1) + _dot(tri_f, g2) + _dot(tri_f, g3)
    cs_b = _dot(tri_b, g1) + _dot(tri_b, g2) + _dot(tri_b, g3)
    col = lax.broadcasted_iota(jnp.int32, (_GATE_ROWS, 128), 1)
    o_ref[0] = jnp.where(col < 2 * DN_V_HEADS, beta, jnp.where(col < 3 * DN_V_HEADS, cs_f, cs_b))


def _dn_project(n, w_in, conv_w, a_log, dt_bias, i):
    tn, sub = 256, 2
    tb = tn * sub

    def conv_proj(col0, width, l2norm, name):
        blk0 = col0 // tb
        return _proj_call(functools.partial(_dn_conv_kernel, tn=tn, sub=sub, l2norm=l2norm), n, w_in, i, blk0,
                          width, tb, (conv_w,),
                          [pl.BlockSpec((1, DN_SHORT_K, tb), lambda b, j: (i, 0, blk0 + j))], BF16, name)

    qk = conv_proj(0, 2 * DN_QK_W, True, "dn_qk")
    vv = conv_proj(2 * DN_QK_W, DN_V_W, False, "dn_v")
    z = _proj_call(_plain_kernel, n, w_in, i, DN_CONV_CH // tn, DN_V_W, tn, (), [], BF16, "dn_z")
    zeros = jnp.zeros((2 * DN_V_HEADS,), F32)
    alog = jnp.concatenate([zeros, a_log[i].reshape(-1)]).reshape(1, 128)
    dtb = jnp.concatenate([zeros, dt_bias[i].reshape(-1)]).reshape(1, 128)
    gates = pl.pallas_call(
        _dn_gate_kernel,
        grid=(NB, S // _GATE_ROWS),
        in_specs=[
            pl.BlockSpec((1, _GATE_ROWS, D), lambda b, c: (b, c, 0)),
            pl.BlockSpec((1, D, 128), lambda b, c: (i, 0, (DN_CONV_CH + DN_V_W) // 128)),
            pl.BlockSpec((1, 128), lambda b, c: (0, 0)),
            pl.BlockSpec((1, 128), lambda b, c: (0, 0)),
        ],
        out_specs=pl.BlockSpec((1, _GATE_ROWS, 128), lambda b, c: (b, c, 0)),
        out_shape=jax.ShapeDtypeStruct((NB, S, 128), F32),
        scratch_shapes=[pltpu.VMEM((D, 128), BF16)],
        compiler_params=_params(("arbitrary", "arbitrary")),
        name="dn_gates",
    )(n, w_in, alog, dtb)
    return qk, vv, z, gates


_DN_G = 32
_DN_HG = DN_V_HEADS // _DN_G
_DN_REP = DN_V_HEADS // DN_K_HEADS


def _bmm(a, b):
    return lax.dot_general(a, b, (((2,), (1,)), ((0,), (0,))), preferred_element_type=F32)


def _bmm_nt(a, b):
    return lax.dot_general(a, b, (((2,), (2,)), ((0,), (0,))), preferred_element_type=F32)


def _bmm_tn(a, b):
    return lax.dot_general(a, b, (((1,), (1,)), ((0,), (0,))), preferred_element_type=F32)


def _hi_lo(x):
    hi = x.astype(BF16).astype(F32)
    return hi, x - hi


def _bmm_f32_lhs(a, b):
    ah, al = _hi_lo(a)
    lhs = jnp.concatenate([ah, al], axis=2).astype(BF16)
    return _bmm(lhs, jnp.concatenate([b, b], axis=1))


def _bmm_f32(a, b):
    ah, al = _hi_lo(a)
    bh, bl = _hi_lo(b)
    lhs = jnp.concatenate([ah, al, ah], axis=2).astype(BF16)
    rhs = jnp.concatenate([bh, bh, bl], axis=1).astype(BF16)
    return _bmm(lhs, rhs)


def _dn_scan_kernel(q_ref, k_ref, v_ref, g1_ref, g2_ref, o_ref, s_ref):
    d = pl.program_id(1)
    step = pl.program_id(3)
    ng, nk = _DN_G, _DN_G // _DN_REP

    @pl.when(step == 0)
    def _():
        s_ref[...] = jnp.zeros_like(s_ref)

    fwd = d == 0
    ii = lax.broadcasted_iota(jnp.int32, (CHUNK, CHUNK), 0)
    jj = lax.broadcasted_iota(jnp.int32, (CHUNK, CHUNK), 1)
    later = (ii - jj) * jnp.where(fwd, 1, -1)
    incl = later >= 0
    strict = jnp.where(later > 0, 1.0, 0.0)
    eye = jnp.where(ii == jj, 1.0, 0.0)

    def same_block(size):
        return jnp.where((ii // size) == (jj // size), 1.0, 0.0)

    g1 = g1_ref[0, 0, 0]
    g2 = g2_ref[0, 0, 0, 0]
    q3 = jnp.stack([q_ref[0, :, i * DN_DK:(i + 1) * DN_DK] for i in range(nk)])
    k3 = jnp.stack([k_ref[0, :, i * DN_DK:(i + 1) * DN_DK] for i in range(nk)])
    v3 = jnp.stack([v_ref[0, :, e * DN_DV:(e + 1) * DN_DV] for e in range(ng)])
    beta_r = jnp.stack([g2[e:e + 1, :] for e in range(ng)])
    gc_r = jnp.stack([g2[ng + e:ng + e + 1, :] for e in range(ng)])
    beta_c = jnp.stack([jnp.broadcast_to(g1[:, e:e + 1], (CHUNK, CHUNK)) for e in range(ng)])
    gc_c = jnp.stack([jnp.broadcast_to(g1[:, ng + e:ng + e + 1], (CHUNK, DN_DK)) for e in range(ng)])
    g_last = jnp.where(fwd, gc_r[:, :, CHUNK - 1:CHUNK], gc_r[:, :, 0:1])

    decay = jnp.exp(jnp.where(incl, gc_c[:, :, :CHUNK] - gc_r, -jnp.inf))
    kk = jnp.repeat(_bmm_nt(k3, k3), _DN_REP, axis=0)
    qk = jnp.repeat(_bmm_nt(q3, k3), _DN_REP, axis=0)
    a = kk * (beta_c * decay) * strict

    def mm(x, y):
        return _bmm(x.astype(BF16), y.astype(BF16))

    d4 = a * same_block(4)
    imd = eye - d4
    t = imd + mm(imd, mm(d4, d4))
    size = 4
    while size < CHUNK:
        n_s = a * (same_block(2 * size) - same_block(size))
        t = t - mm(t, mm(n_s, t))
        size *= 2
    t = t + mm(t, eye - t - _bmm_f32(a, t))

    k_v = jnp.repeat(k3, _DN_REP, axis=0)
    tb = t * beta_r
    u = _bmm_f32_lhs(tb, v3)
    w = _bmm_f32_lhs(tb * jnp.exp(gc_r), k_v)

    st = s_ref[...]
    stb = st.astype(BF16)
    qg = jnp.repeat(q3, _DN_REP, axis=0) * jnp.exp(gc_c).astype(BF16)
    ws = _bmm(jnp.concatenate([w.astype(BF16), qg], axis=1), stb)
    v_new = u - ws[:, :CHUNK]
    vnb = v_new.astype(BF16)
    o = ws[:, CHUNK:] + _bmm((qk * decay).astype(BF16), vnb)
    for e in range(ng):
        o_ref[0, 0, :, e * DN_DV:(e + 1) * DN_DV] = o[e].astype(o_ref.dtype)
    kd = k_v * jnp.exp(g_last - gc_c).astype(BF16)
    s_ref[...] = st * jnp.exp(g_last) + _bmm_tn(kd, vnb)


def _dn_scan(qk, vv, gates):
    hpg = DN_V_HEADS // _DN_HG
    kw = (DN_K_HEADS // _DN_HG) * DN_DK
    vw = hpg * DN_DV
    bt = gates[..., :2 * DN_V_HEADS].reshape(NB, S, 2, _DN_HG, hpg)
    gc = gates[..., 2 * DN_V_HEADS:].reshape(NB, S, 2, _DN_HG, hpg)
    g1 = jnp.transpose(jnp.concatenate([bt, gc], axis=-1), (0, 2, 3, 1, 4))
    g2 = jnp.swapaxes(g1.reshape(NB, 2, _DN_HG, NCH, CHUNK, 2 * hpg), -1, -2)

    def chunk(d, s):
        return jnp.where(d == 0, s, jnp.where(s < CTX_CH, CTX_CH - 1 - s, NCH + CTX_CH - 1 - s))

    return pl.pallas_call(
        _dn_scan_kernel,
        grid=(NB, 2, _DN_HG, NCH),
        in_specs=[
            pl.BlockSpec((1, CHUNK, kw), lambda b, d, g, s: (b, chunk(d, s), g)),
            pl.BlockSpec((1, CHUNK, kw), lambda b, d, g, s: (b, chunk(d, s), DN_QK_W // kw + g)),
            pl.BlockSpec((1, CHUNK, vw), lambda b, d, g, s: (b, chunk(d, s), g)),
            pl.BlockSpec((1, 1, 1, CHUNK, 2 * hpg), lambda b, d, g, s: (b, d, g, chunk(d, s), 0)),
            pl.BlockSpec((1, 1, 1, 1, 2 * hpg, CHUNK), lambda b, d, g, s: (b, d, g, chunk(d, s), 0, 0)),
        ],
        out_specs=pl.BlockSpec((1, 1, CHUNK, vw), lambda b, d, g, s: (b, d, chunk(d, s), g)),
        out_shape=jax.ShapeDtypeStruct((NB, 2, S, DN_V_W), BF16),
        scratch_shapes=[pltpu.VMEM((hpg, DN_DK, DN_DV), F32)],
        compiler_params=_params(("parallel", "parallel", "parallel", "arbitrary")),
        name="dn_scan",
    )(qk, qk, vv, g1, g2)


def _dn_out_kernel(o_ref, z_ref, g_ref, y_ref):
    g = g_ref[...]
    for hh in range(o_ref.shape[-1] // DN_DV):
        sl = slice(hh * DN_DV, (hh + 1) * DN_DV)
        o = o_ref[0, 0, :, sl].astype(F32) + o_ref[0, 1, :, sl].astype(F32)
        o = o * lax.rsqrt(jnp.mean(o * o, axis=-1, keepdims=True) + EPS) * g
        y_ref[0, :, sl] = (o * _silu(z_ref[0, :, sl].astype(F32))).astype(BF16)


def _dn_gated_norm(o, z, norm_g):
    ts, tw = S // 3, 1024
    return pl.pallas_call(
        _dn_out_kernel,
        grid=(NB, S // ts, DN_V_W // tw),
        in_specs=[
            pl.BlockSpec((1, 2, ts, tw), lambda b, t, c: (b, 0, t, c)),
            pl.BlockSpec((1, ts, tw), lambda b, t, c: (b, t, c)),
            pl.BlockSpec((1, DN_DV), lambda b, t, c: (0, 0)),
        ],
        out_specs=pl.BlockSpec((1, ts, tw), lambda b, t, c: (b, t, c)),
        out_shape=jax.ShapeDtypeStruct((NB, S, DN_V_W), BF16),
        compiler_params=_params(("parallel", "parallel", "parallel")),
        name="dn_gated_norm",
    )(o, z, norm_g.reshape(1, DN_DV))


def kernel(x, c, ctx, c_ctx, w_mod, b_mod, norm1_g, norm2_g, ffn_w_up, ffn_conv_w, ffn_conv_b, ffn_w_down, even_w_in, even_w_out, attn_q_norm_g, attn_k_norm_g, attn_sink, conv_dw_w, conv_dw_b, conv_ln_g, conv_ln_b, dn_w_in, dn_conv_w, dn_a_log, dn_dt_bias, dn_norm_g, dn_w_out):
    cvec = jnp.concatenate([c, c_ctx[None, :], jnp.zeros((8 - NB - 1, D), F32)], axis=0)
    mods = _modulation(cvec, w_mod, b_mod).reshape(DEPTH, 8, N_MOD, D)
    rope = _rope_tables()
    h = jnp.concatenate([ctx, x], axis=1)
    m = NB * S
    for layer in range(DEPTH):
        i = layer // 2
        sh1, sc1, g1, sh2, sc2, g2 = [mods[layer, :, j, :] for j in range(N_MOD)]
        n = _norm_mod(h, norm1_g[layer], sc1.reshape(8, 1, D), sh1.reshape(8, 1, D))
        hf = h.reshape(m, D)
        if layer % 2 == 0:
            qk, v, glu = _even_proj(n, even_w_in, attn_q_norm_g[i], attn_k_norm_g[i], rope, i)
            att = _attention(qk, v, attn_sink, i)
            cv = _conformer(glu, conv_dw_w, conv_dw_b, conv_ln_g, conv_ln_b, i)
            hf = _mm_res([att.reshape(m, A_Q_W), cv.reshape(m, CONV_CH)], even_w_out, i, hf, g1, "even_out")
        else:
            qk, vv, z, gates = _dn_project(n, dn_w_in, dn_conv_w, dn_a_log, dn_dt_bias, i)
            o = _dn_scan(qk, vv, gates)
            y = _dn_gated_norm(o, z, dn_norm_g[i])
            hf = _mm_res([y.reshape(m, DN_V_W)], dn_w_out, i, hf, g1, "dn_out")
        h = hf.reshape(NB, S, D)
        n2 = _norm_mod(h, norm2_g[layer], sc2.reshape(8, 1, D), sh2.reshape(8, 1, D))
        f = _ffn_up(n2, ffn_w_up, ffn_conv_w, ffn_conv_b, layer)
        h = _mm_res([f.reshape(m, D_FF)], ffn_w_down, layer, h.reshape(m, D), g2, "ffn_down").reshape(NB, S, D)
    return h[:, CTX:, :]
```
